```python
import jax
import jax.numpy as jnp
from jax import lax
import numpy as np

D_MODEL = 1024
BATCH = 2
SEQ = 8192
DEPTH = 1

GRID_W = 64
CTX_LEN = 256
N_MOD = 6
EPS = 1e-6
SGU_WIDTH = 1024
SGU_GROUPS = 8
SGU_GROUP_DIM = SGU_WIDTH // SGU_GROUPS
CHUNK = 128
MLA_HEADS = 8
QK_NOPE = 128
QK_ROPE = 64
V_DIM = 128
Q_RANK = 256
KV_RANK = 128
ROPE_THETA = 10000.0
Q_BLOCK = 128
PEER_HEADS = 8
N_KEYS = 128
N_EXPERTS = N_KEYS * N_KEYS
PEER_QK = 256
PEER_HALF = PEER_QK // 2
TOPK_HALF = 16
PEER_TOPK = 16
PEER_BLOCK = 128
OFF_U = 0
OFF_V = OFF_U + SGU_WIDTH
OFF_Q = OFF_V + SGU_WIDTH
OFF_KV = OFF_Q + Q_RANK
OFF_KR = OFF_KV + KV_RANK
OFF_GA = OFF_KR + QK_ROPE
OFF_GB = OFF_GA + D_MODEL
IN_WIDTH = OFF_GB + D_MODEL

kernel_name = 'hybrid_sgu_mla_peer_dit_layer'


def rms_norm(x, g):
    xf = x.astype(jnp.float32)
    y = xf * lax.rsqrt(jnp.mean(xf * xf, axis=-1, keepdims=True) + EPS)
    return (y * g.astype(jnp.float32)).astype(x.dtype)


def modulate(h, shift, scale):
    return h * (1 + scale) + shift


def rope_1d(x, pos):
    quarter = x.shape[-1] // 2
    freqs = ROPE_THETA ** (-jnp.arange(quarter, dtype=jnp.float32) / quarter)
    ang = pos.astype(jnp.float32)[:, None] * freqs[None, :]
    shape = (x.shape[1],) + (1,) * (x.ndim - 3) + (quarter,)
    cos = jnp.cos(ang).reshape(shape).astype(x.dtype)
    sin = jnp.sin(ang).reshape(shape).astype(x.dtype)
    x1, x2 = x[..., :quarter], x[..., quarter:]
    return jnp.concatenate([x1 * cos - x2 * sin, x1 * sin + x2 * cos], axis=-1)


def axial_rope(x, rows, cols):
    half = x.shape[-1] // 2
    return jnp.concatenate([rope_1d(x[..., :half], rows), rope_1d(x[..., half:], cols)], axis=-1)


def mla_q(cq, g_q, w_uq):
    b, t, _ = cq.shape
    q = (rms_norm(cq, g_q) @ w_uq).reshape(b, t, MLA_HEADS, QK_NOPE + QK_ROPE)
    return q[..., :QK_NOPE], q[..., QK_NOPE:]


def mla_kv(ckv, g_kv, w_ukv):
    b, t, _ = ckv.shape
    kv = (rms_norm(ckv, g_kv) @ w_ukv).reshape(b, t, MLA_HEADS, QK_NOPE + V_DIM)
    return kv[..., :QK_NOPE], kv[..., QK_NOPE:]


def attend(qn, qr, kn, kr, v):
    b, t, h, _ = qn.shape
    nb = t // Q_BLOCK
    scale = (QK_NOPE + QK_ROPE) ** -0.5

    def block(args):
        qn_b, qr_b = args
        s = jnp.einsum('bqhd,bkhd->bhqk', qn_b, kn) + jnp.einsum('bqhr,bkr->bhqk', qr_b, kr)
        p = jax.nn.softmax(s.astype(jnp.float32) * scale, axis=-1).astype(v.dtype)
        return jnp.einsum('bhqk,bkhd->bqhd', p, v)

    qn_b = qn.reshape(b, nb, Q_BLOCK, h, QK_NOPE).swapaxes(0, 1)
    qr_b = qr.reshape(b, nb, Q_BLOCK, h, QK_ROPE).swapaxes(0, 1)
    o = lax.map(block, (qn_b, qr_b))
    return o.swapaxes(0, 1).reshape(b, t, h * V_DIM)


def chunk_sgu(u, v, g_sgu, w_s, b_s):
    b, t, _ = u.shape
    v = rms_norm(jax.nn.gelu(v), g_sgu).reshape(b, t // CHUNK, CHUNK, SGU_GROUPS, SGU_GROUP_DIM)
    sv = jnp.einsum('gpq,bnqgc->bnpgc', w_s, v) + b_s.T[:, :, None]
    return jax.nn.gelu(u) * sv.reshape(b, t, SGU_WIDTH)


def merge(o_a, o_b, g_a, g_b, w_pa, w_pb, w_o):
    return (jax.nn.sigmoid(g_a) * (o_a @ w_pa) + jax.nn.sigmoid(g_b) * (o_b @ w_pb)) @ w_o


def peer(h, w_pq, k1, k2, u_exp, v_exp):
    b, t, d = h.shape
    blocks = h.reshape(-1, PEER_BLOCK, d)

    def block(xb):
        q = (xb @ w_pq).reshape(PEER_BLOCK, PEER_HEADS, PEER_QK)
        s1 = jnp.einsum('thd,kd->thk', q[..., :PEER_HALF], k1).astype(jnp.float32)
        s2 = jnp.einsum('thd,kd->thk', q[..., PEER_HALF:], k2).astype(jnp.float32)
        s1t, i1 = lax.top_k(s1, TOPK_HALF)
        s2t, i2 = lax.top_k(s2, TOPK_HALF)
        n_cand = TOPK_HALF * TOPK_HALF
        cand = (s1t[..., :, None] + s2t[..., None, :]).reshape(PEER_BLOCK, PEER_HEADS, n_cand)
        cidx = (i1[..., :, None] * N_KEYS + i2[..., None, :]).reshape(PEER_BLOCK, PEER_HEADS, n_cand)
        top_s, pos = lax.top_k(cand, PEER_TOPK)
        idx = jnp.take_along_axis(cidx, pos, axis=-1)
        gw = jax.nn.softmax(top_s, axis=-1).astype(xb.dtype)
        act = jax.nn.gelu(jnp.einsum('thkd,td->thk', u_exp[idx], xb))
        return jnp.einsum('thk,thkd->td', gw * act, v_exp[idx])

    return lax.map(block, blocks).reshape(b, t, d)


def setup_inputs(seed: int = 0) -> dict:
    key = jax.random.key(seed)
    ks = jax.random.split(key, 26)
    L, D = DEPTH, D_MODEL

    def nrm(k, shape, scale):
        return jax.random.normal(k, shape, jnp.float32) * scale

    def gain(k, shape):
        return 1.0 + 0.05 * jax.random.normal(k, shape, jnp.float32)

    return {
        'x': nrm(ks[0], (BATCH, SEQ, D), 1.0),
        'c': nrm(ks[1], (BATCH, D), 1.0),
        'ctx': nrm(ks[2], (BATCH, CTX_LEN, D), 1.0),
        'c_ctx': nrm(ks[3], (D,), 1.0),
        'w_mod': nrm(ks[4], (L, D, N_MOD * D), 0.5 * D ** -0.5),
        'b_mod': nrm(ks[5], (L, N_MOD * D), 0.02),
        'g_pre1': gain(ks[6], (L, D)),
        'g_post1': gain(ks[7], (L, D)),
        'g_pre2': gain(ks[8], (L, D)),
        'g_post2': gain(ks[9], (L, D)),
        'w_in': nrm(ks[10], (L, D, IN_WIDTH), D ** -0.5),
        'g_q': gain(ks[11], (L, Q_RANK)),
        'g_kv': gain(ks[12], (L, KV_RANK)),
        'w_uq': nrm(ks[13], (L, Q_RANK, MLA_HEADS * (QK_NOPE + QK_ROPE)), Q_RANK ** -0.5),
        'w_ukv': nrm(ks[14], (L, KV_RANK, MLA_HEADS * (QK_NOPE + V_DIM)), KV_RANK ** -0.5),
        'g_sgu': gain(ks[15], (L, SGU_WIDTH)),
        'w_s': nrm(ks[16], (L, SGU_GROUPS, CHUNK, CHUNK), CHUNK ** -0.5),
        'b_s': nrm(ks[17], (L, SGU_GROUPS, CHUNK), 0.02),
        'w_pa': nrm(ks[18], (L, SGU_WIDTH, D), SGU_WIDTH ** -0.5),
        'w_pb': nrm(ks[19], (L, MLA_HEADS * V_DIM, D), (MLA_HEADS * V_DIM) ** -0.5),
        'w_o': nrm(ks[20], (L, D, D), D ** -0.5),
        'w_pq': nrm(ks[21], (L, D, PEER_HEADS * PEER_QK), D ** -0.5),
        'k1': nrm(ks[22], (L, N_KEYS, PEER_HALF), PEER_HALF ** -0.5),
        'k2': nrm(ks[23], (L, N_KEYS, PEER_HALF), PEER_HALF ** -0.5),
        'u_exp': nrm(ks[24], (L, N_EXPERTS, D), D ** -0.5),
        'v_exp': nrm(ks[25], (L, N_EXPERTS, D), D ** -0.5),
    }


def reference(x, c, ctx, c_ctx, w_mod, b_mod, g_pre1, g_post1, g_pre2, g_post2, w_in, g_q, g_kv,
              w_uq, w_ukv, g_sgu, w_s, b_s, w_pa, w_pb, w_o, w_pq, k1, k2, u_exp, v_exp):
    seq_len = x.shape[1]
    n_rows = seq_len // GRID_W
    rr, cc = jnp.meshgrid(jnp.arange(n_rows), jnp.arange(GRID_W), indexing='ij')
    rows, cols = rr.reshape(-1), cc.reshape(-1)
    for l in range(DEPTH):
        update_ctx = l < DEPTH - 1
        sh1x, sc1x, gt1x, sh2x, sc2x, gt2x = jnp.split(
            (jax.nn.silu(c) @ w_mod[l] + b_mod[l])[:, None, :], N_MOD, axis=-1)
        sh1c, sc1c, gt1c, sh2c, sc2c, gt2c = jnp.split(
            (jax.nn.silu(c_ctx) @ w_mod[l] + b_mod[l])[None, None, :], N_MOD, axis=-1)

        hx = modulate(rms_norm(x, g_pre1[l]), sh1x, sc1x)
        hc = modulate(rms_norm(ctx, g_pre1[l]), sh1c, sc1c)
        zx = hx @ w_in[l]
        zc_kv = hc @ w_in[l][:, OFF_KV:OFF_GA]
        kn_c, v_c = mla_kv(zc_kv[..., :KV_RANK], g_kv[l], w_ukv[l])
        kr_c = zc_kv[..., KV_RANK:]
        qn_x, qr_x = mla_q(zx[..., OFF_Q:OFF_KV], g_q[l], w_uq[l])
        qr_x = axial_rope(qr_x, rows, cols)
        kn_x, v_x = mla_kv(zx[..., OFF_KV:OFF_KR], g_kv[l], w_ukv[l])
        kr_x = axial_rope(zx[..., OFF_KR:OFF_GA], rows, cols)
        o_b = attend(qn_x, qr_x,
                     jnp.concatenate([kn_x, kn_c], axis=1),
                     jnp.concatenate([kr_x, kr_c], axis=1),
                     jnp.concatenate([v_x, v_c], axis=1))
        o_a = chunk_sgu(zx[..., OFF_U:OFF_V], zx[..., OFF_V:OFF_Q], g_sgu[l], w_s[l], b_s[l])
        y = merge(o_a, o_b, zx[..., OFF_GA:OFF_GB], zx[..., OFF_GB:], w_pa[l], w_pb[l], w_o[l])
        x = x + gt1x * rms_norm(y, g_post1[l])
        if update_ctx:
            zc = hc @ w_in[l][:, :OFF_KV]
            gc = hc @ w_in[l][:, OFF_GA:]
            qn_c, qr_c = mla_q(zc[..., OFF_Q:OFF_KV], g_q[l], w_uq[l])
            o_bc = attend(qn_c, qr_c, kn_c, kr_c, v_c)
            o_ac = chunk_sgu(zc[..., OFF_U:OFF_V], zc[..., OFF_V:OFF_Q], g_sgu[l], w_s[l], b_s[l])
            yc = merge(o_ac, o_bc, gc[..., :D_MODEL], gc[..., D_MODEL:], w_pa[l], w_pb[l], w_o[l])
            ctx = ctx + gt1c * rms_norm(yc, g_post1[l])

        hx2 = modulate(rms_norm(x, g_pre2[l]), sh2x, sc2x)
        x = x + gt2x * rms_norm(peer(hx2, w_pq[l], k1[l], k2[l], u_exp[l], v_exp[l]), g_post2[l])
        if update_ctx:
            hc2 = modulate(rms_norm(ctx, g_pre2[l]), sh2c, sc2c)
            ctx = ctx + gt2c * rms_norm(peer(hc2, w_pq[l], k1[l], k2[l], u_exp[l], v_exp[l]), g_post2[l])
    return x
```

```python
import functools

import jax
import jax.numpy as jnp
from jax import lax
from jax.experimental import pallas as pl
from jax.experimental.pallas import tpu as pltpu

F32 = jnp.float32
BF16 = jnp.bfloat16

D_MODEL = 1024
N_MOD = 6
EPS = 1e-6
GRID_W = 64
SGU_WIDTH = 1024
SGU_GROUPS = 8
CHUNK = 128
MLA_HEADS = 8
QK_NOPE = 128
QK_ROPE = 64
V_DIM = 128
Q_RANK = 256
KV_RANK = 128
ROPE_THETA = 10000.0
PEER_HEADS = 8
N_KEYS = 128
PEER_HALF = 128
PEER_TOPK = 16

LANE = 128
HEAD_PAD = 2 * LANE
VMEM_LIMIT = 56 * 1024 * 1024

C_U = 0
C_V = C_U + SGU_WIDTH
C_Q = C_V + SGU_WIDTH
C_KV = C_Q + Q_RANK
C_KR = C_KV + KV_RANK
C_KT = C_KR + LANE
C_GA = C_KT + LANE
C_GB = C_GA + D_MODEL
C_END = C_GB + D_MODEL

NT_DIMS = (((1,), (1,)), ((), ()))


def _rms(x, g):
    return x * lax.rsqrt(jnp.mean(x * x, axis=-1, keepdims=True) + EPS) * g


def _dot(a, b):
    return jnp.dot(a, b, preferred_element_type=F32)


def _params(*sem):
    return pltpu.CompilerParams(dimension_semantics=sem, vmem_limit_bytes=VMEM_LIMIT)


def _mod_kernel(c_ref, w_ref, b_ref, o_ref):
    c = c_ref[...]
    a = (c * jax.nn.sigmoid(c)).astype(BF16)
    o_ref[...] = _dot(a, w_ref[...].astype(BF16)) + b_ref[...]


def _mod_call(cvec, w_mod, b_mod):
    n = w_mod.shape[1]
    bn = n // 4
    return pl.pallas_call(
        _mod_kernel,
        grid=(n // bn,),
        in_specs=[pl.BlockSpec((8, D_MODEL), lambda j: (0, 0)),
                  pl.BlockSpec((D_MODEL, bn), lambda j: (0, j)),
                  pl.BlockSpec((1, bn), lambda j: (0, j))],
        out_specs=pl.BlockSpec((8, bn), lambda j: (0, j)),
        out_shape=jax.ShapeDtypeStruct((8, n), F32),
        compiler_params=_params("arbitrary"),
        name="mod",
    )(cvec, w_mod, b_mod)


def _kv_path(hb, win_ref, gkv_ref, wkv_ref, cos, sin, k_out, v_out):
    rows = hb.shape[0]
    ckv = _dot(hb, win_ref[:, C_KV:C_KR])
    ckvn = _rms(ckv, gkv_ref[...]).astype(BF16)
    kn = _dot(ckvn, wkv_ref[:, :MLA_HEADS * QK_NOPE])
    vv = _dot(ckvn, wkv_ref[:, MLA_HEADS * QK_NOPE:])
    kr = _dot(hb, win_ref[:, C_KR:C_KT])
    if cos is not None:
        kr = kr * cos + _dot(hb, win_ref[:, C_KT:C_GA]) * sin
    krb = kr.astype(BF16)
    ones = jnp.ones((rows, LANE), BF16)
    for h in range(MLA_HEADS):
        k_out[0, :, h * HEAD_PAD:h * HEAD_PAD + LANE] = kn[:, h * LANE:(h + 1) * LANE].astype(BF16)
        k_out[0, :, h * HEAD_PAD + LANE:(h + 1) * HEAD_PAD] = krb
        v_out[0, :, h * HEAD_PAD:h * HEAD_PAD + LANE] = vv[:, h * LANE:(h + 1) * LANE].astype(BF16)
        v_out[0, :, h * HEAD_PAD + LANE:(h + 1) * HEAD_PAD] = ones


def _inproj_kernel(x_ref, mod_ref, gpre_ref, win_ref, gq_ref, gkv_ref, wq_ref, wkv_ref, gsgu_ref,
                   ws_ref, bs_ref, wpa_ref, cos_ref, sin_ref,
                   q_out, k_out, v_out, pa_out, sgb_out, *, q_scale):
    tm = x_ref.shape[0]
    x = x_ref[...]
    hb = (_rms(x, gpre_ref[...]) * (1.0 + mod_ref[0, 1:2, :]) + mod_ref[0, 0:1, :]).astype(BF16)
    cos = cos_ref[...]
    sin = sin_ref[...]

    gu = jax.nn.gelu(_dot(hb, win_ref[:, C_U:C_V]))
    vn = _rms(jax.nn.gelu(_dot(hb, win_ref[:, C_V:C_Q])), gsgu_ref[...]).astype(BF16)
    chunks = []
    for ci in range(tm // CHUNK):
        cols = [_dot(ws_ref[g], vn[ci * CHUNK:(ci + 1) * CHUNK, g * LANE:(g + 1) * LANE])
                for g in range(SGU_GROUPS)]
        chunks.append(jnp.concatenate(cols, axis=1) + bs_ref[...])
    oa = (gu * jnp.concatenate(chunks, axis=0)).astype(BF16)
    pa_out[...] = (jax.nn.sigmoid(_dot(hb, win_ref[:, C_GA:C_GB])) * _dot(oa, wpa_ref[...])).astype(BF16)
    sgb_out[...] = jax.nn.sigmoid(_dot(hb, win_ref[:, C_GB:C_END])).astype(BF16)

    cqn = _rms(_dot(hb, win_ref[:, C_Q:C_KV]), gq_ref[...]).astype(BF16)
    nw = MLA_HEADS * LANE
    qn = _dot(cqn, wq_ref[:, :nw])
    qr = _dot(cqn, wq_ref[:, nw:2 * nw])
    qt = _dot(cqn, wq_ref[:, 2 * nw:])
    for h in range(MLA_HEADS):
        sl = slice(h * LANE, (h + 1) * LANE)
        q_out[:, h * HEAD_PAD:h * HEAD_PAD + LANE] = (qn[:, sl] * q_scale).astype(BF16)
        q_out[:, h * HEAD_PAD + LANE:(h + 1) * HEAD_PAD] = (
            (qr[:, sl] * cos + qt[:, sl] * sin) * q_scale).astype(BF16)

    _kv_path(hb, win_ref, gkv_ref, wkv_ref, cos, sin, k_out, v_out)


def _ctxkv_kernel(c_ref, mod_ref, gpre_ref, win_ref, gkv_ref, wkv_ref, k_in, v_in, k_out, v_out):
    del k_in, v_in
    hb = (_rms(c_ref[0], gpre_ref[...]) * (1.0 + mod_ref[0, 1:2, :]) + mod_ref[0, 0:1, :]).astype(BF16)
    _kv_path(hb, win_ref, gkv_ref, wkv_ref, None, None, k_out, v_out)


def _const(shape):
    nd = len(shape)
    return pl.BlockSpec(shape, lambda *_: (0,) * nd)


def _inproj_call(x2, mod3, gpre, win, gq, gkv, wq, wkv, gsgu, ws, bs, wpa, cos, sin, *, batch, seq, ktot, tm):
    n = batch * seq
    tiles = seq // tm
    hw = MLA_HEADS * HEAD_PAD
    row = lambda i: (i, 0)
    kern = functools.partial(_inproj_kernel, q_scale=float((QK_NOPE + QK_ROPE) ** -0.5))
    return pl.pallas_call(
        kern,
        grid=(n // tm,),
        in_specs=[pl.BlockSpec((tm, D_MODEL), row),
                  pl.BlockSpec((1, N_MOD, D_MODEL), lambda i: (i // tiles, 0, 0)),
                  _const(gpre.shape), _const(win.shape), _const(gq.shape), _const(gkv.shape),
                  _const(wq.shape), _const(wkv.shape), _const(gsgu.shape), _const(ws.shape),
                  _const(bs.shape), _const(wpa.shape),
                  pl.BlockSpec((tm, LANE), lambda i: (i % tiles, 0)),
                  pl.BlockSpec((tm, LANE), lambda i: (i % tiles, 0))],
        out_specs=[pl.BlockSpec((tm, hw), row),
                   pl.BlockSpec((1, tm, hw), lambda i: (i // tiles, i % tiles, 0)),
                   pl.BlockSpec((1, tm, hw), lambda i: (i // tiles, i % tiles, 0)),
                   pl.BlockSpec((tm, D_MODEL), row),
                   pl.BlockSpec((tm, D_MODEL), row)],
        out_shape=[jax.ShapeDtypeStruct((n, hw), BF16),
                   jax.ShapeDtypeStruct((batch, ktot, hw), BF16),
                   jax.ShapeDtypeStruct((batch, ktot, hw), BF16),
                   jax.ShapeDtypeStruct((n, D_MODEL), BF16),
                   jax.ShapeDtypeStruct((n, D_MODEL), BF16)],
        compiler_params=_params("arbitrary"),
        name="inproj",
    )(x2, mod3, gpre, win, gq, gkv, wq, wkv, gsgu, ws, bs, wpa, cos, sin)


def _ctxkv_call(ctx, mod3, gpre, win, gkv, wkv, k_all, v_all, *, batch, seq):
    ctx_len = ctx.shape[1]
    hw = MLA_HEADS * HEAD_PAD
    tail = lambda b: (b, seq // ctx_len, 0)
    return pl.pallas_call(
        _ctxkv_kernel,
        grid=(batch,),
        in_specs=[pl.BlockSpec((1, ctx_len, D_MODEL), lambda b: (b, 0, 0)),
                  pl.BlockSpec((1, N_MOD, D_MODEL), lambda b: (batch, 0, 0)),
                  _const(gpre.shape), _const(win.shape), _const(gkv.shape), _const(wkv.shape),
                  pl.BlockSpec(memory_space=pl.ANY), pl.BlockSpec(memory_space=pl.ANY)],
        out_specs=[pl.BlockSpec((1, ctx_len, hw), tail), pl.BlockSpec((1, ctx_len, hw), tail)],
        out_shape=[jax.ShapeDtypeStruct(k_all.shape, BF16), jax.ShapeDtypeStruct(v_all.shape, BF16)],
        input_output_aliases={6: 0, 7: 1},
        compiler_params=_params("arbitrary"),
        name="ctxkv",
    )(ctx, mod3, gpre, win, gkv, wkv, k_all, v_all)


def _attn_kernel(q_ref, k_ref, v_ref, o_ref, *, tk):
    q = q_ref[...]
    tq = q.shape[0]
    m = jnp.full((tq, 1), -jnp.inf, F32)
    acc = jnp.zeros((tq, HEAD_PAD), F32)
    for c in range(k_ref.shape[1] // tk):
        s = lax.dot_general(q, k_ref[0, c * tk:(c + 1) * tk, :], NT_DIMS, preferred_element_type=F32)
        m_new = jnp.maximum(m, jnp.max(s, axis=1, keepdims=True))
        p = jnp.exp(s - m_new).astype(BF16)
        acc = acc * jnp.exp(m - m_new) + _dot(p, v_ref[0, c * tk:(c + 1) * tk, :])
        m = m_new
    o_ref[...] = (acc[:, :V_DIM] / acc[:, V_DIM:]).astype(BF16)


def _attn_call(q, k_all, v_all, *, batch, seq, tq, tk):
    ktot = k_all.shape[1]
    tiles = seq // tq
    return pl.pallas_call(
        functools.partial(_attn_kernel, tk=tk),
        grid=(batch, MLA_HEADS, tiles),
        in_specs=[pl.BlockSpec((tq, HEAD_PAD), lambda b, h, i: (b * tiles + i, h)),
                  pl.BlockSpec((1, ktot, HEAD_PAD), lambda b, h, i: (b, 0, h)),
                  pl.BlockSpec((1, ktot, HEAD_PAD), lambda b, h, i: (b, 0, h))],
        out_specs=pl.BlockSpec((tq, V_DIM), lambda b, h, i: (b * tiles + i, h)),
        out_shape=jax.ShapeDtypeStruct((batch * seq, MLA_HEADS * V_DIM), BF16),
        compiler_params=_params("arbitrary", "arbitrary", "arbitrary"),
        name="attn",
    )(q, k_all, v_all)


def _post_kernel(x_ref, ob_ref, pa_ref, sgb_ref, mod_ref, wpb_ref, wo_ref, gpost_ref, gpre2_ref,
                 wpq_ref, k1_ref, k2_ref, x1_out, hx2_out, s1_out, s2_out):
    t = _dot(ob_ref[...], wpb_ref[...])
    mix = (pa_ref[...].astype(F32) + sgb_ref[...].astype(F32) * t).astype(BF16)
    y = _dot(mix, wo_ref[...])
    x1 = x_ref[...] + mod_ref[0, 2:3, :] * _rms(y, gpost_ref[...])
    x1_out[...] = x1
    hx2 = (_rms(x1, gpre2_ref[...]) * (1.0 + mod_ref[0, 4:5, :]) + mod_ref[0, 3:4, :]).astype(BF16)
    hx2_out[...] = hx2
    qp = _dot(hx2, wpq_ref[...]).astype(BF16)
    k1 = k1_ref[...]
    k2 = k2_ref[...]
    for h in range(PEER_HEADS):
        base = h * 2 * PEER_HALF
        s1_out[h] = lax.dot_general(k1, qp[:, base:base + PEER_HALF], NT_DIMS, preferred_element_type=F32)
        s2_out[h] = lax.dot_general(k2, qp[:, base + PEER_HALF:base + 2 * PEER_HALF], NT_DIMS,
                                    preferred_element_type=F32)


def _post_call(x2, ob, pa, sgb, mod3, wpb, wo, gpost, gpre2, wpq, k1, k2, *, seq, tm):
    n = x2.shape[0]
    tiles = seq // tm
    row = lambda i: (i, 0)
    tok = lambda i: (0, 0, i)
    return pl.pallas_call(
        _post_kernel,
        grid=(n // tm,),
        in_specs=[pl.BlockSpec((tm, D_MODEL), row), pl.BlockSpec((tm, D_MODEL), row),
                  pl.BlockSpec((tm, D_MODEL), row), pl.BlockSpec((tm, D_MODEL), row),
                  pl.BlockSpec((1, N_MOD, D_MODEL), lambda i: (i // tiles, 0, 0)),
                  _const(wpb.shape), _const(wo.shape), _const(gpost.shape), _const(gpre2.shape),
                  _const(wpq.shape), _const(k1.shape), _const(k2.shape)],
        out_specs=[pl.BlockSpec((tm, D_MODEL), row), pl.BlockSpec((tm, D_MODEL), row),
                   pl.BlockSpec((PEER_HEADS, N_KEYS, tm), tok), pl.BlockSpec((PEER_HEADS, N_KEYS, tm), tok)],
        out_shape=[jax.ShapeDtypeStruct((n, D_MODEL), F32), jax.ShapeDtypeStruct((n, D_MODEL), BF16),
                   jax.ShapeDtypeStruct((PEER_HEADS, N_KEYS, n), F32),
                   jax.ShapeDtypeStruct((PEER_HEADS, N_KEYS, n), F32)],
        compiler_params=_params("arbitrary"),
        name="post",
    )(x2, ob, pa, sgb, mod3, wpb, wo, gpost, gpre2, wpq, k1, k2)


def _topk_kernel(s1_ref, s2_ref, r2_out, c1_out, coef_out, e2_out, t1_scr, t2_scr):
    neg = -jnp.inf

    def body(h, carry):
        s1 = s1_ref[h]
        s2 = s2_ref[h]
        s = s1
        for a in range(PEER_TOPK):
            m = jnp.max(s, axis=0, keepdims=True)
            t1_scr[a:a + 1, :] = m
            s = jnp.where(s == m, neg, s)
        s = s2
        r2 = jnp.full(s2.shape, float(PEER_TOPK), F32)
        for b in range(PEER_TOPK):
            m = jnp.max(s, axis=0, keepdims=True)
            t2_scr[b:b + 1, :] = m
            hit = s == m
            r2 = jnp.where(hit, float(b), r2)
            s = jnp.where(hit, neg, s)
        t2 = t2_scr[...]
        cand = jnp.concatenate([t1_scr[a:a + 1, :] + t2 for a in range(PEER_TOPK)], axis=0)
        c = cand
        for k in range(PEER_TOPK):
            tau = jnp.max(c, axis=0, keepdims=True)
            if k + 1 < PEER_TOPK:
                c = jnp.where(c == tau, neg, c)
        m1 = t1_scr[0:1, :]
        m2 = t2_scr[0:1, :]
        keep = cand >= tau
        z = jnp.sum(jnp.where(keep, jnp.exp(cand - (m1 + m2)), 0.0), axis=0, keepdims=True)
        c1 = jnp.zeros(s1.shape, F32)
        for a in range(PEER_TOPK):
            cnt = jnp.sum(jnp.where(keep[a * PEER_TOPK:(a + 1) * PEER_TOPK], 1.0, 0.0), axis=0, keepdims=True)
            c1 = jnp.where(s1 == t1_scr[a:a + 1, :], cnt, c1)
        r2_out[h] = r2
        c1_out[h] = c1
        coef_out[h] = jnp.exp(s1 - m1) / z
        e2_out[h] = jnp.exp(s2 - m2)
        return carry

    lax.fori_loop(0, PEER_HEADS, body, 0)


def _topk_call(s1t, s2t, *, tl):
    n = s1t.shape[2]
    blk = pl.BlockSpec((PEER_HEADS, N_KEYS, tl), lambda i: (0, 0, i))
    shp = jax.ShapeDtypeStruct((PEER_HEADS, N_KEYS, n), F32)
    return pl.pallas_call(
        _topk_kernel,
        grid=(n // tl,),
        in_specs=[blk, blk],
        out_specs=[blk, blk, blk, blk],
        out_shape=[shp, shp, shp, shp],
        scratch_shapes=[pltpu.VMEM((PEER_TOPK, tl), F32), pltpu.VMEM((PEER_TOPK, tl), F32)],
        compiler_params=_params("arbitrary"),
        name="topk",
    )(s1t, s2t)


def _dense_kernel(hx2_ref, u_ref, vt_ref, r2_ref, e2_ref, c1_ref, coef_ref, x1_ref, mod_ref, gpost_ref,
                  o_ref, w_scr, acc_scr):
    j = pl.program_id(1)
    te = u_ref.shape[0]
    tm = hx2_ref.shape[0]

    @pl.when(j == 0)
    def _():
        acc_scr[...] = jnp.zeros_like(acc_scr)

    a_t = lax.dot_general(u_ref[...], hx2_ref[...], NT_DIMS, preferred_element_type=F32)
    for g in range(te // N_KEYS):
        for c in range(tm // LANE):
            ls = slice(c * LANE, (c + 1) * LANE)
            gate = jnp.zeros((N_KEYS, LANE), F32)
            for h in range(PEER_HEADS):
                hit = r2_ref[h, :, ls] < c1_ref[h, g:g + 1, ls]
                gate = gate + jnp.where(hit, e2_ref[h, :, ls] * coef_ref[h, g:g + 1, ls], 0.0)
            act = jax.nn.gelu(a_t[g * N_KEYS:(g + 1) * N_KEYS, ls])
            w_scr[g * N_KEYS:(g + 1) * N_KEYS, ls] = (act * gate).astype(BF16)
    acc_scr[...] += _dot(vt_ref[...], w_scr[...])

    @pl.when(j == pl.num_programs(1) - 1)
    def _():
        peer = acc_scr[...].T
        o_ref[...] = x1_ref[...] + mod_ref[0, 5:6, :] * _rms(peer, gpost_ref[...])


def _dense_call(hx2, u_b, vt_b, r2, e2, c1, coef, x1, mod3, gpost2, *, seq, tm, te):
    n = hx2.shape[0]
    ne = u_b.shape[0]
    tiles = seq // tm
    tok = lambda i, j: (0, 0, i)
    return pl.pallas_call(
        _dense_kernel,
        grid=(n // tm, ne // te),
        in_specs=[pl.BlockSpec((tm, D_MODEL), lambda i, j: (i, 0)),
                  pl.BlockSpec((te, D_MODEL), lambda i, j: (j, 0)),
                  pl.BlockSpec((D_MODEL, te), lambda i, j: (0, j)),
                  pl.BlockSpec((PEER_HEADS, N_KEYS, tm), tok),
                  pl.BlockSpec((PEER_HEADS, N_KEYS, tm), tok),
                  pl.BlockSpec((PEER_HEADS, te // N_KEYS, tm), lambda i, j: (0, j, i)),
                  pl.BlockSpec((PEER_HEADS, te // N_KEYS, tm), lambda i, j: (0, j, i)),
                  pl.BlockSpec((tm, D_MODEL), lambda i, j: (i, 0)),
                  pl.BlockSpec((1, N_MOD, D_MODEL), lambda i, j: (i // tiles, 0, 0)),
                  pl.BlockSpec((1, D_MODEL), lambda i, j: (0, 0))],
        out_specs=pl.BlockSpec((tm, D_MODEL), lambda i, j: (i, 0)),
        out_shape=jax.ShapeDtypeStruct((n, D_MODEL), F32),
        scratch_shapes=[pltpu.VMEM((te, tm), BF16), pltpu.VMEM((D_MODEL, tm), F32)],
        compiler_params=_params("arbitrary", "arbitrary"),
        name="dense",
    )(hx2, u_b, vt_b, r2, e2, c1, coef, x1, mod3, gpost2)


def _rot_cols(w):
    lead = w.shape[:-1]
    w4 = w.reshape(lead + (2, 2, QK_ROPE // 4))
    return jnp.concatenate([-w4[..., 1:2, :], w4[..., 0:1, :]], axis=-2).reshape(w.shape)


def _pad_lanes(w):
    return jnp.concatenate([w, jnp.zeros(w.shape[:-1] + (LANE - w.shape[-1],), w.dtype)], axis=-1)


def _rope_tables(seq, dtype):
    quarter = QK_ROPE // 4
    t = jnp.arange(seq)
    freqs = ROPE_THETA ** (-jnp.arange(quarter, dtype=jnp.float32) / quarter)
    ang_r = (t // GRID_W).astype(jnp.float32)[:, None] * freqs[None, :]
    ang_c = (t % GRID_W).astype(jnp.float32)[:, None] * freqs[None, :]
    cos = jnp.concatenate([jnp.cos(ang_r), jnp.cos(ang_r), jnp.cos(ang_c), jnp.cos(ang_c)], axis=-1)
    sin = jnp.concatenate([jnp.sin(ang_r), jnp.sin(ang_r), jnp.sin(ang_c), jnp.sin(ang_c)], axis=-1)
    return _pad_lanes(cos).astype(dtype), _pad_lanes(sin).astype(dtype)


def _tiles(seq, ktot):
    tm = 256 if seq % 256 == 0 else CHUNK
    tk = ktot
    for cand in (1408, 1024, 768, 512, 384, 256, 128):
        if ktot % cand == 0:
            tk = cand
            break
    return dict(tm=tm, tq=tm, tk=tk, tl=256 if seq % 256 == 0 else LANE,
                td=512 if seq % 512 == 0 else tm, te=1024)


def kernel(x, c, ctx, c_ctx, w_mod, b_mod, g_pre1, g_post1, g_pre2, g_post2, w_in, g_q, g_kv, w_uq, w_ukv,
           g_sgu, w_s, b_s, w_pa, w_pb, w_o, w_pq, k1, k2, u_exp, v_exp):
    batch, seq, d = x.shape
    ctx_len = ctx.shape[1]
    depth = w_mod.shape[0]
    assert depth == 1 and d == D_MODEL and batch + 1 <= 8
    assert seq % ctx_len == 0 and seq % CHUNK == 0
    ktot = seq + ctx_len
    tl = _tiles(seq, ktot)
    l = 0

    cvec = jnp.zeros((8, d), F32).at[:batch].set(c).at[batch].set(c_ctx)
    mod = _mod_call(cvec, w_mod[l], b_mod[l][None, :])
    mod3 = mod[:batch + 1].reshape(batch + 1, N_MOD, d)

    wi = w_in[l]
    off_q = 2 * SGU_WIDTH
    off_kv = off_q + Q_RANK
    off_kr = off_kv + KV_RANK
    off_ga = off_kr + QK_ROPE
    w_kr = wi[:, off_kr:off_ga]
    win = jnp.concatenate([wi[:, :off_kr], _pad_lanes(w_kr), _pad_lanes(_rot_cols(w_kr)), wi[:, off_ga:]],
                          axis=1).astype(BF16)
    wq3 = w_uq[l].reshape(Q_RANK, MLA_HEADS, QK_NOPE + QK_ROPE)
    wq_rope = wq3[:, :, QK_NOPE:]
    wq = jnp.concatenate([wq3[:, :, :QK_NOPE].reshape(Q_RANK, -1),
                          _pad_lanes(wq_rope).reshape(Q_RANK, -1),
                          _pad_lanes(_rot_cols(wq_rope)).reshape(Q_RANK, -1)], axis=1).astype(BF16)
    wkv3 = w_ukv[l].reshape(KV_RANK, MLA_HEADS, QK_NOPE + V_DIM)
    wkv = jnp.concatenate([wkv3[:, :, :QK_NOPE].reshape(KV_RANK, -1),
                           wkv3[:, :, QK_NOPE:].reshape(KV_RANK, -1)], axis=1).astype(BF16)
    bs = jnp.repeat(b_s[l].T, SGU_WIDTH // SGU_GROUPS, axis=1)
    cos, sin = _rope_tables(seq, x.dtype)
    row = lambda g: g[l][None, :]

    x2 = x.reshape(batch * seq, d)
    q, k_all, v_all, pa, sgb = _inproj_call(
        x2, mod3, row(g_pre1), win, row(g_q), row(g_kv), wq, wkv, row(g_sgu), w_s[l].astype(BF16), bs,
        w_pa[l].astype(BF16), cos, sin, batch=batch, seq=seq, ktot=ktot, tm=tl["tm"])
    k_all, v_all = _ctxkv_call(ctx, mod3, row(g_pre1), win, row(g_kv), wkv, k_all, v_all, batch=batch, seq=seq)

    ob = _attn_call(q, k_all, v_all, batch=batch, seq=seq, tq=tl["tq"], tk=tl["tk"])

    x1, hx2, s1t, s2t = _post_call(
        x2, ob, pa, sgb, mod3, w_pb[l].astype(BF16), w_o[l].astype(BF16), row(g_post1), row(g_pre2),
        w_pq[l].astype(BF16), k1[l].astype(BF16), k2[l].astype(BF16), seq=seq, tm=tl["tm"])

    r2, c1, coef, e2 = _topk_call(s1t, s2t, tl=tl["tl"])

    out = _dense_call(hx2, u_exp[l].astype(BF16), v_exp[l].astype(BF16).T, r2, e2, c1, coef, x1, mod3,
                      row(g_post2), seq=seq, tm=tl["td"], te=tl["te"])
    return out.reshape(batch, seq, d)
```

```python
import functools

import jax
import jax.numpy as jnp
from jax import lax
from jax.experimental import pallas as pl
from jax.experimental.pallas import tpu as pltpu

F32 = jnp.float32
BF16 = jnp.bfloat16

D_MODEL = 1024
N_MOD = 6
EPS = 1e-6
GRID_W = 64
SGU_WIDTH = 1024
SGU_GROUPS = 8
CHUNK = 128
MLA_HEADS = 8
QK_NOPE = 128
QK_ROPE = 64
V_DIM = 128
Q_RANK = 256
KV_RANK = 128
ROPE_THETA = 10000.0
PEER_HEADS = 8
N_KEYS = 128
PEER_HALF = 128
PEER_TOPK = 16

LOG2E = 1.4426950408889634
LANE = 128
BF16_ROWS = 16
HEAD_PAD = 2 * LANE
VMEM_LIMIT = 56 * 1024 * 1024

C_U = 0
C_V = C_U + SGU_WIDTH
C_Q = C_V + SGU_WIDTH
C_KV = C_Q + Q_RANK
C_KR = C_KV + KV_RANK
C_KT = C_KR + LANE
C_GA = C_KT + LANE
C_GB = C_GA + D_MODEL
C_END = C_GB + D_MODEL

NT_DIMS = (((1,), (1,)), ((), ()))

def _rms(x, g):
    return x * lax.rsqrt(jnp.mean(x * x, axis=-1, keepdims=True) + EPS) * g


def _dot(a, b):
    return jnp.dot(a, b, preferred_element_type=F32)


def _gelu_tanh(x):
    c0 = 0.7978845608028654
    c1 = c0 * 0.044715
    half = 0.5 * x
    return half * jnp.tanh(x * (c0 + c1 * (x * x))) + half


def _params(*sem):
    return pltpu.CompilerParams(dimension_semantics=sem, vmem_limit_bytes=VMEM_LIMIT)


def _mod_kernel(c_ref, w_ref, b_ref, o_ref):
    c = c_ref[...]
    a = (c * jax.nn.sigmoid(c)).astype(BF16)
    o_ref[...] = _dot(a, w_ref[...].astype(BF16)) + b_ref[...]


def _mod_call(cvec, w_mod, b_mod):
    n = w_mod.shape[1]
    bn = n // 4
    return pl.pallas_call(
        _mod_kernel,
        grid=(n // bn,),
        in_specs=[pl.BlockSpec((8, D_MODEL), lambda j: (0, 0)),
                  pl.BlockSpec((D_MODEL, bn), lambda j: (0, j)),
                  pl.BlockSpec((1, bn), lambda j: (0, j))],
        out_specs=pl.BlockSpec((8, bn), lambda j: (0, j)),
        out_shape=jax.ShapeDtypeStruct((8, n), F32),
        compiler_params=_params("arbitrary"),
        name="mod",
    )(cvec, w_mod, b_mod)


def _kv_path(hb, win_ref, gkv_ref, wkv_ref, cos, sin, k_out, v_out):
    rows = hb.shape[0]
    ckv = _dot(hb, win_ref[:, C_KV:C_KR])
    ckvn = _rms(ckv, gkv_ref[...]).astype(BF16)
    kn = _dot(ckvn, wkv_ref[:, :MLA_HEADS * QK_NOPE])
    vv = _dot(ckvn, wkv_ref[:, MLA_HEADS * QK_NOPE:])
    kr = _dot(hb, win_ref[:, C_KR:C_KT])
    if cos is not None:
        kr = kr * cos + _dot(hb, win_ref[:, C_KT:C_GA]) * sin
    krb = kr.astype(BF16)
    ones = jnp.ones((rows, LANE), BF16)
    for h in range(MLA_HEADS):
        k_out[0, :, h * HEAD_PAD:h * HEAD_PAD + LANE] = kn[:, h * LANE:(h + 1) * LANE].astype(BF16)
        k_out[0, :, h * HEAD_PAD + LANE:(h + 1) * HEAD_PAD] = krb
        v_out[0, :, h * HEAD_PAD:h * HEAD_PAD + LANE] = vv[:, h * LANE:(h + 1) * LANE].astype(BF16)
        v_out[0, :, h * HEAD_PAD + LANE:(h + 1) * HEAD_PAD] = ones


def _inproj_kernel(x_ref, mod_ref, gpre_ref, win_ref, gq_ref, gkv_ref, wq_ref, wkv_ref, gsgu_ref,
                   ws_ref, bs_ref, wpa_ref, cos_ref, sin_ref,
                   q_out, k_out, v_out, pa_out, sgb_out, *, q_scale):
    tm = x_ref.shape[0]
    x = x_ref[...]
    hb = (_rms(x, gpre_ref[...]) * (1.0 + mod_ref[0, 1:2, :]) + mod_ref[0, 0:1, :]).astype(BF16)
    cos = cos_ref[...]
    sin = sin_ref[...]

    gu = jax.nn.gelu(_dot(hb, win_ref[:, C_U:C_V]))
    vn = _rms(jax.nn.gelu(_dot(hb, win_ref[:, C_V:C_Q])), gsgu_ref[...]).astype(BF16)
    chunks = []
    for ci in range(tm // CHUNK):
        cols = [_dot(ws_ref[g], vn[ci * CHUNK:(ci + 1) * CHUNK, g * LANE:(g + 1) * LANE])
                for g in range(SGU_GROUPS)]
        chunks.append(jnp.concatenate(cols, axis=1) + bs_ref[...])
    oa = (gu * jnp.concatenate(chunks, axis=0)).astype(BF16)
    pa_out[...] = (jax.nn.sigmoid(_dot(hb, win_ref[:, C_GA:C_GB])) * _dot(oa, wpa_ref[...])).astype(BF16)
    sgb_out[...] = jax.nn.sigmoid(_dot(hb, win_ref[:, C_GB:C_END])).astype(BF16)

    cqn = _rms(_dot(hb, win_ref[:, C_Q:C_KV]), gq_ref[...]).astype(BF16)
    nw = MLA_HEADS * LANE
    qn = _dot(cqn, wq_ref[:, :nw])
    qr = _dot(cqn, wq_ref[:, nw:2 * nw])
    qt = _dot(cqn, wq_ref[:, 2 * nw:])
    for h in range(MLA_HEADS):
        sl = slice(h * LANE, (h + 1) * LANE)
        q_out[:, h * HEAD_PAD:h * HEAD_PAD + LANE] = (qn[:, sl] * q_scale).astype(BF16)
        q_out[:, h * HEAD_PAD + LANE:(h + 1) * HEAD_PAD] = (
            (qr[:, sl] * cos + qt[:, sl] * sin) * q_scale).astype(BF16)

    _kv_path(hb, win_ref, gkv_ref, wkv_ref, cos, sin, k_out, v_out)


def _ctxkv_kernel(c_ref, mod_ref, gpre_ref, win_ref, gkv_ref, wkv_ref, k_in, v_in, k_out, v_out):
    del k_in, v_in
    hb = (_rms(c_ref[0], gpre_ref[...]) * (1.0 + mod_ref[0, 1:2, :]) + mod_ref[0, 0:1, :]).astype(BF16)
    _kv_path(hb, win_ref, gkv_ref, wkv_ref, None, None, k_out, v_out)


def _const(shape):
    nd = len(shape)
    return pl.BlockSpec(shape, lambda *_: (0,) * nd)


def _inproj_call(x2, mod3, gpre, win, gq, gkv, wq, wkv, gsgu, ws, bs, wpa, cos, sin, *, batch, seq, ktot, tm):
    n = batch * seq
    tiles = seq // tm
    hw = MLA_HEADS * HEAD_PAD
    row = lambda i: (i, 0)
    kern = functools.partial(_inproj_kernel, q_scale=float((QK_NOPE + QK_ROPE) ** -0.5 * LOG2E))
    return pl.pallas_call(
        kern,
        grid=(n // tm,),
        in_specs=[pl.BlockSpec((tm, D_MODEL), row),
                  pl.BlockSpec((1, N_MOD, D_MODEL), lambda i: (i // tiles, 0, 0)),
                  _const(gpre.shape), _const(win.shape), _const(gq.shape), _const(gkv.shape),
                  _const(wq.shape), _const(wkv.shape), _const(gsgu.shape), _const(ws.shape),
                  _const(bs.shape), _const(wpa.shape),
                  pl.BlockSpec((tm, LANE), lambda i: (i % tiles, 0)),
                  pl.BlockSpec((tm, LANE), lambda i: (i % tiles, 0))],
        out_specs=[pl.BlockSpec((tm, hw), row),
                   pl.BlockSpec((1, tm, hw), lambda i: (i // tiles, i % tiles, 0)),
                   pl.BlockSpec((1, tm, hw), lambda i: (i // tiles, i % tiles, 0)),
                   pl.BlockSpec((tm, D_MODEL), row),
                   pl.BlockSpec((tm, D_MODEL), row)],
        out_shape=[jax.ShapeDtypeStruct((n, hw), BF16),
                   jax.ShapeDtypeStruct((batch, ktot, hw), BF16),
                   jax.ShapeDtypeStruct((batch, ktot, hw), BF16),
                   jax.ShapeDtypeStruct((n, D_MODEL), BF16),
                   jax.ShapeDtypeStruct((n, D_MODEL), BF16)],
        compiler_params=_params("arbitrary"),
        name="inproj",
    )(x2, mod3, gpre, win, gq, gkv, wq, wkv, gsgu, ws, bs, wpa, cos, sin)


def _ctxkv_call(ctx, mod3, gpre, win, gkv, wkv, k_all, v_all, *, batch, seq):
    ctx_len = ctx.shape[1]
    hw = MLA_HEADS * HEAD_PAD
    tail = lambda b: (b, seq // ctx_len, 0)
    return pl.pallas_call(
        _ctxkv_kernel,
        grid=(batch,),
        in_specs=[pl.BlockSpec((1, ctx_len, D_MODEL), lambda b: (b, 0, 0)),
                  pl.BlockSpec((1, N_MOD, D_MODEL), lambda b: (batch, 0, 0)),
                  _const(gpre.shape), _const(win.shape), _const(gkv.shape), _const(wkv.shape),
                  pl.BlockSpec(memory_space=pl.ANY), pl.BlockSpec(memory_space=pl.ANY)],
        out_specs=[pl.BlockSpec((1, ctx_len, hw), tail), pl.BlockSpec((1, ctx_len, hw), tail)],
        out_shape=[jax.ShapeDtypeStruct(k_all.shape, BF16), jax.ShapeDtypeStruct(v_all.shape, BF16)],
        input_output_aliases={6: 0, 7: 1},
        compiler_params=_params("arbitrary"),
        name="ctxkv",
    )(ctx, mod3, gpre, win, gkv, wkv, k_all, v_all)


def _attn_kernel(q_ref, k_ref, v_ref, o_ref, *, tk):
    q = q_ref[...]
    tq = q.shape[0]
    m = jnp.full((tq, 1), -jnp.inf, F32)
    acc = jnp.zeros((tq, HEAD_PAD), F32)
    for c in range(k_ref.shape[1] // tk):
        s = lax.dot_general(q, k_ref[0, c * tk:(c + 1) * tk, :], NT_DIMS, preferred_element_type=F32)
        m_new = jnp.maximum(m, jnp.max(s, axis=1, keepdims=True))
        p = jnp.exp2(s - m_new).astype(BF16)
        acc = acc * jnp.exp2(m - m_new) + _dot(p, v_ref[0, c * tk:(c + 1) * tk, :])
        m = m_new
    o_ref[...] = (acc[:, :V_DIM] / acc[:, V_DIM:]).astype(BF16)


def _attn_call(q, k_all, v_all, *, batch, seq, tq, tk):
    ktot = k_all.shape[1]
    tiles = seq // tq
    return pl.pallas_call(
        functools.partial(_attn_kernel, tk=tk),
        grid=(batch, MLA_HEADS, tiles),
        in_specs=[pl.BlockSpec((tq, HEAD_PAD), lambda b, h, i: (b * tiles + i, h)),
                  pl.BlockSpec((1, ktot, HEAD_PAD), lambda b, h, i: (b, 0, h)),
                  pl.BlockSpec((1, ktot, HEAD_PAD), lambda b, h, i: (b, 0, h))],
        out_specs=pl.BlockSpec((tq, V_DIM), lambda b, h, i: (b * tiles + i, h)),
        out_shape=jax.ShapeDtypeStruct((batch * seq, MLA_HEADS * V_DIM), BF16),
        compiler_params=_params("arbitrary", "arbitrary", "arbitrary"),
        name="attn",
    )(q, k_all, v_all)


def _post_kernel(x_ref, ob_ref, pa_ref, sgb_ref, mod_ref, wpb_ref, wo_ref, gpost_ref, gpre2_ref,
                 wpq_ref, k1_ref, k2_ref, x1_out, hx2_out, s1_out, s2_out):
    t = _dot(ob_ref[...], wpb_ref[...])
    mix = (pa_ref[...].astype(F32) + sgb_ref[...].astype(F32) * t).astype(BF16)
    y = _dot(mix, wo_ref[...])
    x1 = x_ref[...] + mod_ref[0, 2:3, :] * _rms(y, gpost_ref[...])
    x1_out[...] = x1
    hx2 = (_rms(x1, gpre2_ref[...]) * (1.0 + mod_ref[0, 4:5, :]) + mod_ref[0, 3:4, :]).astype(BF16)
    hx2_out[...] = hx2
    qp = _dot(hx2, wpq_ref[...]).astype(BF16)
    k1 = k1_ref[...]
    k2 = k2_ref[...]
    for h in range(PEER_HEADS):
        base = h * 2 * PEER_HALF
        s1_out[h] = lax.dot_general(k1, qp[:, base:base + PEER_HALF], NT_DIMS, preferred_element_type=F32)
        s2_out[h] = lax.dot_general(k2, qp[:, base + PEER_HALF:base + 2 * PEER_HALF], NT_DIMS,
                                    preferred_element_type=F32)


def _post_call(x2, ob, pa, sgb, mod3, wpb, wo, gpost, gpre2, wpq, k1, k2, *, seq, tm):
    n = x2.shape[0]
    tiles = seq // tm
    row = lambda i: (i, 0)
    tok = lambda i: (0, 0, i)
    return pl.pallas_call(
        _post_kernel,
        grid=(n // tm,),
        in_specs=[pl.BlockSpec((tm, D_MODEL), row), pl.BlockSpec((tm, D_MODEL), row),
                  pl.BlockSpec((tm, D_MODEL), row), pl.BlockSpec((tm, D_MODEL), row),
                  pl.BlockSpec((1, N_MOD, D_MODEL), lambda i: (i // tiles, 0, 0)),
                  _const(wpb.shape), _const(wo.shape), _const(gpost.shape), _const(gpre2.shape),
                  _const(wpq.shape), _const(k1.shape), _const(k2.shape)],
        out_specs=[pl.BlockSpec((tm, D_MODEL), row), pl.BlockSpec((tm, D_MODEL), row),
                   pl.BlockSpec((PEER_HEADS, N_KEYS, tm), tok), pl.BlockSpec((PEER_HEADS, N_KEYS, tm), tok)],
        out_shape=[jax.ShapeDtypeStruct((n, D_MODEL), F32), jax.ShapeDtypeStruct((n, D_MODEL), BF16),
                   jax.ShapeDtypeStruct((PEER_HEADS, N_KEYS, n), F32),
                   jax.ShapeDtypeStruct((PEER_HEADS, N_KEYS, n), F32)],
        compiler_params=_params("arbitrary"),
        name="post",
    )(x2, ob, pa, sgb, mod3, wpb, wo, gpost, gpre2, wpq, k1, k2)


def _topk_kernel(s1_ref, s2_ref, r2_out, c1_out, coef_out, e2_out, t1_scr, t2_scr):
    neg = -jnp.inf

    def body(h, carry):
        s1 = s1_ref[h]
        s2 = s2_ref[h]
        s = s1
        for a in range(PEER_TOPK):
            m = jnp.max(s, axis=0, keepdims=True)
            t1_scr[a:a + 1, :] = m
            s = jnp.where(s == m, neg, s)
        s = s2
        r2 = jnp.full(s2.shape, float(PEER_TOPK), F32)
        for b in range(PEER_TOPK):
            m = jnp.max(s, axis=0, keepdims=True)
            t2_scr[b:b + 1, :] = m
            hit = s == m
            r2 = jnp.where(hit, float(b), r2)
            s = jnp.where(hit, neg, s)
        t2 = t2_scr[...]
        cand = jnp.concatenate([t1_scr[a:a + 1, :] + t2 for a in range(PEER_TOPK)], axis=0)
        c = cand
        for k in range(PEER_TOPK):
            tau = jnp.max(c, axis=0, keepdims=True)
            if k + 1 < PEER_TOPK:
                c = jnp.where(c == tau, neg, c)
        m1 = t1_scr[0:1, :]
        m2 = t2_scr[0:1, :]
        keep = cand >= tau
        z = jnp.sum(jnp.where(keep, jnp.exp(cand - (m1 + m2)), 0.0), axis=0, keepdims=True)
        c1 = jnp.zeros(s1.shape, F32)
        for a in range(PEER_TOPK):
            cnt = jnp.sum(jnp.where(keep[a * PEER_TOPK:(a + 1) * PEER_TOPK], 1.0, 0.0), axis=0, keepdims=True)
            c1 = jnp.where(s1 == t1_scr[a:a + 1, :], cnt, c1)
        c1_out[h] = c1
        coef_out[h] = jnp.exp(s1 - m1) / z
        r2_out[h] = r2.astype(BF16)
        e2_out[h] = jnp.exp(s2 - m2).astype(BF16)
        return carry

    lax.fori_loop(0, PEER_HEADS, body, 0)


def _topk_call(s1t, s2t, *, tl):
    n = s1t.shape[2]
    blk = pl.BlockSpec((PEER_HEADS, N_KEYS, tl), lambda i: (0, 0, i))
    shp = jax.ShapeDtypeStruct((PEER_HEADS, N_KEYS, n), F32)
    blk_b = blk
    shp_b = jax.ShapeDtypeStruct((PEER_HEADS, N_KEYS, n), BF16)
    return pl.pallas_call(
        _topk_kernel,
        grid=(n // tl,),
        in_specs=[blk, blk],
        out_specs=[blk_b, blk, blk, blk_b],
        out_shape=[shp_b, shp, shp, shp_b],
        scratch_shapes=[pltpu.VMEM((PEER_TOPK, tl), F32), pltpu.VMEM((PEER_TOPK, tl), F32)],
        compiler_params=_params("arbitrary"),
        name="topk",
    )(s1t, s2t)


def _dense_kernel(hx2_ref, u0_ref, ux_ref, uy_ref, vtx_ref, vty_ref, vtl_ref, r2_ref, e2_ref, c1_ref, coef_ref,
                  x1_ref, mod_ref, gpost_ref, o_ref, ac_scr, a_scr, wc_scr, w_scr, acc_scr, r2_scr, e2_scr):
    k = pl.program_id(1)
    te = ux_ref.shape[0]
    tm = hx2_ref.shape[0]
    nt = N_KEYS // BF16_ROWS
    groups = te // N_KEYS
    zero = jnp.zeros((), BF16)

    def pre(u_ref):
        a_t = lax.dot_general(u_ref[...], hx2_ref[...], NT_DIMS, preferred_element_type=F32)
        return _gelu_tanh(a_t).astype(BF16)

    def mix(a_ref, w_ref, row0):
        for g in range(groups):
            ks = slice(g * N_KEYS, (g + 1) * N_KEYS)
            r = row0 + g
            for c in range(tm // LANE):
                ls = slice(c * LANE, (c + 1) * LANE)
                gate = jnp.zeros((N_KEYS, LANE), BF16)
                for h in range(PEER_HEADS):
                    cnt = jnp.broadcast_to(c1_ref[h, r:r + 1, ls], (BF16_ROWS, LANE)).astype(BF16)
                    cf = jnp.broadcast_to(coef_ref[h, r:r + 1, ls], (BF16_ROWS, LANE)).astype(BF16)
                    hit = r2_scr[h, :, ls] < pltpu.repeat(cnt, nt, axis=0)
                    gate = gate + jnp.where(hit, e2_scr[h, :, ls] * pltpu.repeat(cf, nt, axis=0), zero)
                w_ref[ks, ls] = a_ref[ks, ls] * gate

    @pl.when(k == 0)
    def _():
        acc_scr[...] = jnp.zeros_like(acc_scr)
        wc_scr[0] = jnp.zeros(wc_scr.shape[1:], BF16)
        ac_scr[0] = pre(u0_ref)
        for h in range(PEER_HEADS):
            r2_scr[h] = r2_ref[h]
            e2_scr[h] = e2_ref[h]

    p = lax.rem(k, 2)
    q = 1 - p
    mix(ac_scr.at[p], w_scr, 0)
    a_scr[...] = pre(ux_ref)
    acc_scr[...] += _dot(vtx_ref[...], wc_scr[p])
    mix(a_scr, wc_scr.at[q], groups)
    ac_scr[q] = pre(uy_ref)
    acc_scr[...] += _dot(vty_ref[...], w_scr[...])

    @pl.when(k == pl.num_programs(1) - 1)
    def _():
        peer = (acc_scr[...] + _dot(vtl_ref[...], wc_scr[q])).T
        o_ref[...] = x1_ref[...] + mod_ref[0, 5:6, :] * _rms(peer, gpost_ref[...])


def _dense_call(hx2, u_b, vt_b, r2, e2, c1, coef, x1, mod3, gpost2, *, seq, tm, te):
    n = hx2.shape[0]
    ne = u_b.shape[0]
    tiles = seq // tm
    nb = ne // te
    assert nb % 2 == 0
    tok = lambda i, k: (0, 0, i)
    u_blk = lambda fn: pl.BlockSpec((te, D_MODEL), lambda i, k: (fn(k), 0))
    vt_blk = lambda fn: pl.BlockSpec((D_MODEL, te), lambda i, k: (0, fn(k)))
    pair_rows = 2 * te // N_KEYS
    return pl.pallas_call(
        _dense_kernel,
        grid=(n // tm, nb // 2),
        in_specs=[pl.BlockSpec((tm, D_MODEL), lambda i, k: (i, 0)),
                  u_blk(lambda k: 0),
                  u_blk(lambda k: 2 * k + 1),
                  u_blk(lambda k: jnp.minimum(2 * k + 2, nb - 1)),
                  vt_blk(lambda k: jnp.maximum(2 * k - 1, 0)),
                  vt_blk(lambda k: 2 * k),
                  vt_blk(lambda k: nb - 1),
                  pl.BlockSpec((PEER_HEADS, N_KEYS, tm), tok),
                  pl.BlockSpec((PEER_HEADS, N_KEYS, tm), tok),
                  pl.BlockSpec((PEER_HEADS, pair_rows, tm), lambda i, k: (0, k, i)),
                  pl.BlockSpec((PEER_HEADS, pair_rows, tm), lambda i, k: (0, k, i)),
                  pl.BlockSpec((tm, D_MODEL), lambda i, k: (i, 0)),
                  pl.BlockSpec((1, N_MOD, D_MODEL), lambda i, k: (i // tiles, 0, 0)),
                  pl.BlockSpec((1, D_MODEL), lambda i, k: (0, 0))],
        out_specs=pl.BlockSpec((tm, D_MODEL), lambda i, k: (i, 0)),
        out_shape=jax.ShapeDtypeStruct((n, D_MODEL), F32),
        scratch_shapes=[pltpu.VMEM((2, te, tm), BF16), pltpu.VMEM((te, tm), BF16),
                        pltpu.VMEM((2, te, tm), BF16), pltpu.VMEM((te, tm), BF16),
                        pltpu.VMEM((D_MODEL, tm), F32),
                        pltpu.VMEM((PEER_HEADS, N_KEYS, tm), BF16), pltpu.VMEM((PEER_HEADS, N_KEYS, tm), BF16)],
        compiler_params=_params("arbitrary", "arbitrary"),
        name="dense",
    )(hx2, u_b, u_b, u_b, vt_b, vt_b, vt_b, r2, e2, c1, coef, x1, mod3, gpost2)


def _rot_cols(w):
    lead = w.shape[:-1]
    w4 = w.reshape(lead + (2, 2, QK_ROPE // 4))
    return jnp.concatenate([-w4[..., 1:2, :], w4[..., 0:1, :]], axis=-2).reshape(w.shape)


def _pad_lanes(w):
    return jnp.concatenate([w, jnp.zeros(w.shape[:-1] + (LANE - w.shape[-1],), w.dtype)], axis=-1)


def _rope_tables(seq, dtype):
    quarter = QK_ROPE // 4
    n_rows = seq // GRID_W
    freqs = ROPE_THETA ** (-jnp.arange(quarter, dtype=jnp.float32) / quarter)
    ang_r = jnp.arange(n_rows).astype(jnp.float32)[:, None] * freqs[None, :]
    ang_c = jnp.arange(GRID_W).astype(jnp.float32)[:, None] * freqs[None, :]

    def table(fn):
        by_row = jnp.broadcast_to(fn(ang_r)[:, None, :], (n_rows, GRID_W, quarter))
        by_col = jnp.broadcast_to(fn(ang_c)[None, :, :], (n_rows, GRID_W, quarter))
        t = jnp.concatenate([by_row, by_row, by_col, by_col], axis=-1).reshape(seq, QK_ROPE)
        return _pad_lanes(t).astype(dtype)

    return table(jnp.cos), table(jnp.sin)


def _tiles(seq, ktot):
    tm = 256 if seq % 256 == 0 else CHUNK
    tk = ktot
    for cand in (1408, 1024, 768, 512, 384, 256, 128):
        if ktot % cand == 0:
            tk = cand
            break
    big = 512 if seq % 512 == 0 else tm
    return dict(tm=tm, tq=big, tk=tk, tl=256 if seq % 256 == 0 else LANE, td=big, te=1024)


def kernel(x, c, ctx, c_ctx, w_mod, b_mod, g_pre1, g_post1, g_pre2, g_post2, w_in, g_q, g_kv, w_uq, w_ukv,
           g_sgu, w_s, b_s, w_pa, w_pb, w_o, w_pq, k1, k2, u_exp, v_exp):
    batch, seq, d = x.shape
    ctx_len = ctx.shape[1]
    depth = w_mod.shape[0]
    assert depth == 1 and d == D_MODEL and batch + 1 <= 8
    assert seq % ctx_len == 0 and seq % CHUNK == 0
    ktot = seq + ctx_len
    tl = _tiles(seq, ktot)
    l = 0

    cvec = jnp.zeros((8, d), F32).at[:batch].set(c).at[batch].set(c_ctx)
    mod = _mod_call(cvec, w_mod[l], b_mod[l][None, :])
    mod3 = mod[:batch + 1].reshape(batch + 1, N_MOD, d)

    wi = w_in[l]
    off_q = 2 * SGU_WIDTH
    off_kv = off_q + Q_RANK
    off_kr = off_kv + KV_RANK
    off_ga = off_kr + QK_ROPE
    w_kr = wi[:, off_kr:off_ga]
    win = jnp.concatenate([wi[:, :off_kr], _pad_lanes(w_kr), _pad_lanes(_rot_cols(w_kr)), wi[:, off_ga:]],
                          axis=1).astype(BF16)
    wq3 = w_uq[l].reshape(Q_RANK, MLA_HEADS, QK_NOPE + QK_ROPE)
    wq_rope = wq3[:, :, QK_NOPE:]
    wq = jnp.concatenate([wq3[:, :, :QK_NOPE].reshape(Q_RANK, -1),
                          _pad_lanes(wq_rope).reshape(Q_RANK, -1),
                          _pad_lanes(_rot_cols(wq_rope)).reshape(Q_RANK, -1)], axis=1).astype(BF16)
    wkv3 = w_ukv[l].reshape(KV_RANK, MLA_HEADS, QK_NOPE + V_DIM)
    wkv = jnp.concatenate([wkv3[:, :, :QK_NOPE].reshape(KV_RANK, -1),
                           wkv3[:, :, QK_NOPE:].reshape(KV_RANK, -1)], axis=1).astype(BF16)
    bs = jnp.repeat(b_s[l].T, SGU_WIDTH // SGU_GROUPS, axis=1)
    cos, sin = _rope_tables(seq, x.dtype)
    row = lambda g: g[l][None, :]

    x2 = x.reshape(batch * seq, d)
    q, k_all, v_all, pa, sgb = _inproj_call(
        x2, mod3, row(g_pre1), win, row(g_q), row(g_kv), wq, wkv, row(g_sgu), w_s[l].astype(BF16), bs,
        w_pa[l].astype(BF16), cos, sin, batch=batch, seq=seq, ktot=ktot, tm=tl["tm"])
    k_all, v_all = _ctxkv_call(ctx, mod3, row(g_pre1), win, row(g_kv), wkv, k_all, v_all, batch=batch, seq=seq)

    ob = _attn_call(q, k_all, v_all, batch=batch, seq=seq, tq=tl["tq"], tk=tl["tk"])

    x1, hx2, s1t, s2t = _post_call(
        x2, ob, pa, sgb, mod3, w_pb[l].astype(BF16), w_o[l].astype(BF16), row(g_post1), row(g_pre2),
        w_pq[l].astype(BF16), k1[l].astype(BF16), k2[l].astype(BF16), seq=seq, tm=tl["tm"])

    r2, c1, coef, e2 = _topk_call(s1t, s2t, tl=tl["tl"])

    out = _dense_call(hx2, u_exp[l].astype(BF16), v_exp[l].astype(BF16).T, r2, e2, c1, coef, x1, mod3,
                      row(g_post2), seq=seq, tm=tl["td"], te=tl["te"])
    return out.reshape(batch, seq, d)
```

```python
import functools

import jax
import jax.numpy as jnp
from jax import lax
from jax.experimental import pallas as pl
from jax.experimental.pallas import tpu as pltpu

F32 = jnp.float32
BF16 = jnp.bfloat16

D_MODEL = 1024
N_MOD = 6
EPS = 1e-6
GRID_W = 64
SGU_WIDTH = 1024
SGU_GROUPS = 8
CHUNK = 128
MLA_HEADS = 8
QK_NOPE = 128
QK_ROPE = 64
V_DIM = 128
Q_RANK = 256
KV_RANK = 128
ROPE_THETA = 10000.0
PEER_HEADS = 8
N_KEYS = 128
PEER_HALF = 128
PEER_TOPK = 16

LOG2E = 1.4426950408889634
LANE = 128
BF16_ROWS = 16
HEAD_PAD = 2 * LANE
VMEM_LIMIT = 56 * 1024 * 1024

C_U = 0
C_V = C_U + SGU_WIDTH
C_Q = C_V + SGU_WIDTH
C_KV = C_Q + Q_RANK
C_KR = C_KV + KV_RANK
C_KT = C_KR + LANE
C_GA = C_KT + LANE
C_GB = C_GA + D_MODEL
C_END = C_GB + D_MODEL

NT_DIMS = (((1,), (1,)), ((), ()))

def _rms(x, g):
    return x * lax.rsqrt(jnp.mean(x * x, axis=-1, keepdims=True) + EPS) * g


def _dot(a, b):
    return jnp.dot(a, b, preferred_element_type=F32)


def _gelu_tanh(x):
    c0 = 0.7978845608028654
    c1 = c0 * 0.044715
    half = 0.5 * x
    return half * jnp.tanh(x * (c0 + c1 * (x * x))) + half


def _params(*sem):
    return pltpu.CompilerParams(dimension_semantics=sem, vmem_limit_bytes=VMEM_LIMIT)


def _mod_kernel(c_ref, w_ref, b_ref, o_ref):
    c = c_ref[...]
    a = (c * jax.nn.sigmoid(c)).astype(BF16)
    o_ref[...] = _dot(a, w_ref[...].astype(BF16)) + b_ref[...]


def _mod_call(cvec, w_mod, b_mod):
    n = w_mod.shape[1]
    bn = n // 4
    return pl.pallas_call(
        _mod_kernel,
        grid=(n // bn,),
        in_specs=[pl.BlockSpec((8, D_MODEL), lambda j: (0, 0)),
                  pl.BlockSpec((D_MODEL, bn), lambda j: (0, j)),
                  pl.BlockSpec((1, bn), lambda j: (0, j))],
        out_specs=pl.BlockSpec((8, bn), lambda j: (0, j)),
        out_shape=jax.ShapeDtypeStruct((8, n), F32),
        compiler_params=_params("arbitrary"),
        name="mod",
    )(cvec, w_mod, b_mod)


def _kv_path(hb, win_ref, gkv_ref, wkv_ref, cos, sin, k_out, v_out):
    rows = hb.shape[0]
    ckv = _dot(hb, win_ref[:, C_KV:C_KR])
    ckvn = _rms(ckv, gkv_ref[...]).astype(BF16)
    kn = _dot(ckvn, wkv_ref[:, :MLA_HEADS * QK_NOPE])
    vv = _dot(ckvn, wkv_ref[:, MLA_HEADS * QK_NOPE:])
    kr = _dot(hb, win_ref[:, C_KR:C_KT])
    if cos is not None:
        kr = kr * cos + _dot(hb, win_ref[:, C_KT:C_GA]) * sin
    krb = kr.astype(BF16)
    ones = jnp.ones((rows, LANE), BF16)
    for h in range(MLA_HEADS):
        k_out[0, :, h * HEAD_PAD:h * HEAD_PAD + LANE] = kn[:, h * LANE:(h + 1) * LANE].astype(BF16)
        k_out[0, :, h * HEAD_PAD + LANE:(h + 1) * HEAD_PAD] = krb
        v_out[0, :, h * HEAD_PAD:h * HEAD_PAD + LANE] = vv[:, h * LANE:(h + 1) * LANE].astype(BF16)
        v_out[0, :, h * HEAD_PAD + LANE:(h + 1) * HEAD_PAD] = ones


def _inproj_kernel(x_ref, mod_ref, gpre_ref, win_ref, gq_ref, gkv_ref, wq_ref, wkv_ref, gsgu_ref,
                   ws_ref, bs_ref, wpa_ref, cos_ref, sin_ref,
                   q_out, k_out, v_out, pa_out, sgb_out, *, q_scale):
    tm = x_ref.shape[0]
    x = x_ref[...]
    hb = (_rms(x, gpre_ref[...]) * (1.0 + mod_ref[0, 1:2, :]) + mod_ref[0, 0:1, :]).astype(BF16)
    cos = cos_ref[...]
    sin = sin_ref[...]

    gu = jax.nn.gelu(_dot(hb, win_ref[:, C_U:C_V]))
    vn = _rms(jax.nn.gelu(_dot(hb, win_ref[:, C_V:C_Q])), gsgu_ref[...]).astype(BF16)
    chunks = []
    for ci in range(tm // CHUNK):
        cols = [_dot(ws_ref[g], vn[ci * CHUNK:(ci + 1) * CHUNK, g * LANE:(g + 1) * LANE])
                for g in range(SGU_GROUPS)]
        chunks.append(jnp.concatenate(cols, axis=1) + bs_ref[...])
    oa = (gu * jnp.concatenate(chunks, axis=0)).astype(BF16)
    pa_out[...] = (jax.nn.sigmoid(_dot(hb, win_ref[:, C_GA:C_GB])) * _dot(oa, wpa_ref[...])).astype(BF16)
    sgb_out[...] = jax.nn.sigmoid(_dot(hb, win_ref[:, C_GB:C_END])).astype(BF16)

    cqn = _rms(_dot(hb, win_ref[:, C_Q:C_KV]), gq_ref[...]).astype(BF16)
    nw = MLA_HEADS * LANE
    qn = _dot(cqn, wq_ref[:, :nw])
    qr = _dot(cqn, wq_ref[:, nw:2 * nw])
    qt = _dot(cqn, wq_ref[:, 2 * nw:])
    for h in range(MLA_HEADS):
        sl = slice(h * LANE, (h + 1) * LANE)
        q_out[:, h * HEAD_PAD:h * HEAD_PAD + LANE] = (qn[:, sl] * q_scale).astype(BF16)
        q_out[:, h * HEAD_PAD + LANE:(h + 1) * HEAD_PAD] = (
            (qr[:, sl] * cos + qt[:, sl] * sin) * q_scale).astype(BF16)

    _kv_path(hb, win_ref, gkv_ref, wkv_ref, cos, sin, k_out, v_out)


def _ctxkv_kernel(c_ref, mod_ref, gpre_ref, win_ref, gkv_ref, wkv_ref, k_out, v_out):
    hb =(_rms(c_ref[0], gpre_ref[...]) * (1.0 + mod_ref[0, 1:2, :]) + mod_ref[0, 0:1, :]).astype(BF16)
    _kv_path(hb, win_ref, gkv_ref, wkv_ref, None, None, k_out, v_out)


def _const(shape):
    nd = len(shape)
    return pl.BlockSpec(shape, lambda *_: (0,) * nd)


def _inproj_call(x2, mod3, gpre, win, gq, gkv, wq, wkv, gsgu, ws, bs, wpa, cos, sin, *, batch, seq, tm):
    n = batch * seq
    tiles = seq // tm
    hw = MLA_HEADS * HEAD_PAD
    row = lambda i: (i, 0)
    kern = functools.partial(_inproj_kernel, q_scale=float((QK_NOPE + QK_ROPE) ** -0.5 * LOG2E))
    return pl.pallas_call(
        kern,
        grid=(n // tm,),
        in_specs=[pl.BlockSpec((tm, D_MODEL), row),
                  pl.BlockSpec((1, N_MOD, D_MODEL), lambda i: (i // tiles, 0, 0)),
                  _const(gpre.shape), _const(win.shape), _const(gq.shape), _const(gkv.shape),
                  _const(wq.shape), _const(wkv.shape), _const(gsgu.shape), _const(ws.shape),
                  _const(bs.shape), _const(wpa.shape),
                  pl.BlockSpec((tm, LANE), lambda i: (i % tiles, 0)),
                  pl.BlockSpec((tm, LANE), lambda i: (i % tiles, 0))],
        out_specs=[pl.BlockSpec((tm, hw), row),
                   pl.BlockSpec((1, tm, hw), lambda i: (i // tiles, i % tiles, 0)),
                   pl.BlockSpec((1, tm, hw), lambda i: (i // tiles, i % tiles, 0)),
                   pl.BlockSpec((tm, D_MODEL), row),
                   pl.BlockSpec((tm, D_MODEL), row)],
        out_shape=[jax.ShapeDtypeStruct((n, hw), BF16),
                   jax.ShapeDtypeStruct((batch, seq, hw), BF16),
                   jax.ShapeDtypeStruct((batch, seq, hw), BF16),
                   jax.ShapeDtypeStruct((n, D_MODEL), BF16),
                   jax.ShapeDtypeStruct((n, D_MODEL), BF16)],
        compiler_params=_params("arbitrary"),
        name="inproj",
    )(x2, mod3, gpre, win, gq, gkv, wq, wkv, gsgu, ws, bs, wpa, cos, sin)


def _ctxkv_call(ctx, mod3, gpre, win, gkv, wkv, *, batch):
    ctx_len = ctx.shape[1]
    hw = MLA_HEADS * HEAD_PAD
    blk = pl.BlockSpec((1, ctx_len, hw), lambda b: (b, 0, 0))
    shp = jax.ShapeDtypeStruct((batch, ctx_len, hw), BF16)
    return pl.pallas_call(
        _ctxkv_kernel,
        grid=(batch,),
        in_specs=[pl.BlockSpec((1, ctx_len, D_MODEL), lambda b: (b, 0, 0)),
                  pl.BlockSpec((1, N_MOD, D_MODEL), lambda b: (batch, 0, 0)),
                  _const(gpre.shape), _const(win.shape), _const(gkv.shape), _const(wkv.shape)],
        out_specs=[blk, blk],
        out_shape=[shp, shp],
        compiler_params=_params("arbitrary"),
        name="ctxkv",
    )(ctx, mod3, gpre, win, gkv, wkv)


def _attn_kernel(q_ref, kx_ref, vx_ref, kc_ref, vc_ref, o_ref, *, tk):
    q = q_ref[...]
    tq = q.shape[0]
    m = jnp.full((tq, 1), -jnp.inf, F32)
    acc = jnp.zeros((tq, HEAD_PAD), F32)
    chunks = [(kx_ref, vx_ref, slice(c * tk, (c + 1) * tk)) for c in range(kx_ref.shape[1] // tk)]
    chunks.append((kc_ref, vc_ref, slice(0, kc_ref.shape[1])))
    for k_ref, v_ref, rows in chunks:
        s = lax.dot_general(q, k_ref[0, rows, :], NT_DIMS, preferred_element_type=F32)
        m_new = jnp.maximum(m, jnp.max(s, axis=1, keepdims=True))
        p = jnp.exp2(s - m_new).astype(BF16)
        acc = acc * jnp.exp2(m - m_new) + _dot(p, v_ref[0, rows, :])
        m = m_new
    o_ref[...] = (acc[:, :V_DIM] / acc[:, V_DIM:]).astype(BF16)


def _attn_call(q, k_x, v_x, k_c, v_c, *, batch, seq, tq, tk):
    ctx_len = k_c.shape[1]
    tiles = seq // tq
    per_head = lambda rows: pl.BlockSpec((1, rows, HEAD_PAD), lambda b, h, i: (b, 0, h))
    return pl.pallas_call(
        functools.partial(_attn_kernel, tk=tk),
        grid=(batch, MLA_HEADS, tiles),
        in_specs=[pl.BlockSpec((tq, HEAD_PAD), lambda b, h, i: (b * tiles + i, h)),
                  per_head(seq), per_head(seq), per_head(ctx_len), per_head(ctx_len)],
        out_specs=pl.BlockSpec((tq, V_DIM), lambda b, h, i: (b * tiles + i, h)),
        out_shape=jax.ShapeDtypeStruct((batch * seq, MLA_HEADS * V_DIM), BF16),
        compiler_params=_params("arbitrary", "arbitrary", "arbitrary"),
        name="attn",
    )(q, k_x, v_x, k_c, v_c)


def _post_kernel(x_ref, ob_ref, pa_ref, sgb_ref, mod_ref, wpb_ref, wo_ref, gpost_ref, gpre2_ref,
                 wpq_ref, k1_ref, k2_ref, x1_out, hx2_out, s1_out, s2_out):
    t = _dot(ob_ref[...], wpb_ref[...])
    mix = (pa_ref[...].astype(F32) + sgb_ref[...].astype(F32) * t).astype(BF16)
    y = _dot(mix, wo_ref[...])
    x1 = x_ref[...] + mod_ref[0, 2:3, :] * _rms(y, gpost_ref[...])
    x1_out[...] = x1
    hx2 = (_rms(x1, gpre2_ref[...]) * (1.0 + mod_ref[0, 4:5, :]) + mod_ref[0, 3:4, :]).astype(BF16)
    hx2_out[...] = hx2
    qp = _dot(hx2, wpq_ref[...]).astype(BF16)
    k1 = k1_ref[...]
    k2 = k2_ref[...]
    for h in range(PEER_HEADS):
        base = h * 2 * PEER_HALF
        s1_out[h] = lax.dot_general(k1, qp[:, base:base + PEER_HALF], NT_DIMS, preferred_element_type=F32)
        s2_out[h] = lax.dot_general(k2, qp[:, base + PEER_HALF:base + 2 * PEER_HALF], NT_DIMS,
                                    preferred_element_type=F32)


def _post_call(x2, ob, pa, sgb, mod3, wpb, wo, gpost, gpre2, wpq, k1, k2, *, seq, tm):
    n = x2.shape[0]
    tiles = seq // tm
    row = lambda i: (i, 0)
    tok = lambda i: (0, 0, i)
    return pl.pallas_call(
        _post_kernel,
        grid=(n // tm,),
        in_specs=[pl.BlockSpec((tm, D_MODEL), row), pl.BlockSpec((tm, D_MODEL), row),
                  pl.BlockSpec((tm, D_MODEL), row), pl.BlockSpec((tm, D_MODEL), row),
                  pl.BlockSpec((1, N_MOD, D_MODEL), lambda i: (i // tiles, 0, 0)),
                  _const(wpb.shape), _const(wo.shape), _const(gpost.shape), _const(gpre2.shape),
                  _const(wpq.shape), _const(k1.shape), _const(k2.shape)],
        out_specs=[pl.BlockSpec((tm, D_MODEL), row), pl.BlockSpec((tm, D_MODEL), row),
                   pl.BlockSpec((PEER_HEADS, N_KEYS, tm), tok), pl.BlockSpec((PEER_HEADS, N_KEYS, tm), tok)],
        out_shape=[jax.ShapeDtypeStruct((n, D_MODEL), F32), jax.ShapeDtypeStruct((n, D_MODEL), BF16),
                   jax.ShapeDtypeStruct((PEER_HEADS, N_KEYS, n), F32),
                   jax.ShapeDtypeStruct((PEER_HEADS, N_KEYS, n), F32)],
        compiler_params=_params("arbitrary"),
        name="post",
    )(x2, ob, pa, sgb, mod3, wpb, wo, gpost, gpre2, wpq, k1, k2)


SUB = 8


def _sort_pairs(n):
    pairs = []

    def merge(lo, hi, r):
        step = r * 2
        if step < hi - lo:
            merge(lo, hi, step)
            merge(lo + r, hi, step)
            pairs.extend((i, i + r) for i in range(lo + r, hi - r, step))
        else:
            pairs.append((lo, lo + r))

    def sort(lo, hi):
        if hi - lo >= 1:
            mid = lo + (hi - lo) // 2
            sort(lo, mid)
            sort(mid + 1, hi)
            merge(lo, hi, 1)

    sort(0, n - 1)
    return pairs


def _cmpx(v, i, j):
    a, b = v[i], v[j]
    if b is None:
        return
    if a is None:
        v[i], v[j] = b, None
        return
    v[i], v[j] = jnp.maximum(a, b), jnp.minimum(a, b)


def _top16_sorted(slabs):
    v = list(slabs) + [None] * (PEER_TOPK - len(slabs))
    for i, j in _sort_pairs(PEER_TOPK):
        _cmpx(v, i, j)
    for shift in (SUB // 2, SUB // 4, SUB // 8):
        rolled = [None if x is None else pltpu.roll(x, shift, axis=0) for x in v]
        merged = []
        for i in range(PEER_TOPK):
            a, b = v[i], rolled[PEER_TOPK - 1 - i]
            merged.append(b if a is None else (a if b is None else jnp.maximum(a, b)))
        v = merged
        d = PEER_TOPK // 2
        while d >= 1:
            for i in range(PEER_TOPK):
                if i & d == 0:
                    _cmpx(v, i, i + d)
            d //= 2
    return v


def _topk_kernel(s1_ref, s2_ref, r2_out, c1_out, coef_out, e2_out):
    nslab = N_KEYS // SUB
    tl = s1_ref.shape[2]
    row = lax.broadcasted_iota(jnp.int32, (SUB, tl), 0)
    row_is = [row == b for b in range(SUB)]

    def spread(vals):
        out = vals[SUB - 1]
        for b in range(SUB - 2, -1, -1):
            out = jnp.where(row_is[b], vals[b], out)
        return out

    def body(h, carry):
        s1 = [s1_ref[h, j * SUB:(j + 1) * SUB, :] for j in range(nslab)]
        s2 = [s2_ref[h, j * SUB:(j + 1) * SUB, :] for j in range(nslab)]
        t1 = _top16_sorted(s1)
        t2 = _top16_sorted(s2)

        t2_lo, t2_hi, t1_hi = spread(t2[:SUB]), spread(t2[SUB:]), spread(t1[SUB:])
        cand = [t1[0] + t2_lo, t1[0] + t2_hi] + [t1[a] + t2_lo for a in range(1, SUB)] + [t1_hi + t2[0]]
        best = _top16_sorted(cand)
        tau = best[PEER_TOPK - 1]
        z = jnp.ones_like(tau)
        for k in range(1, PEER_TOPK):
            z = z + jnp.exp(best[k] - best[0])

        def count(base, bs):
            return sum(jnp.where(base + t2[b] >= tau, 1.0, 0.0) for b in bs)

        extra = [count(t1[0], range(4, 16)), count(t1[1], range(4, 8)), count(t1[2], range(4, 5))]
        inv_z = 1.0 / z
        c1_parts, coef_parts, r2_parts, e2_parts = [], [], [], []
        for j in range(nslab):
            x = s1[j]
            c = count(x, range(4))
            for a in range(3):
                c = c + jnp.where(x == t1[a], extra[a], 0.0)
            c1_parts.append(c)
            coef_parts.append(jnp.exp(x - t1[0]) * inv_z)
            y = s2[j]
            m8 = y < t2[7]
            m4 = y < jnp.where(m8, t2[11], t2[3])
            m2 = y < jnp.where(m8, jnp.where(m4, t2[13], t2[9]), jnp.where(m4, t2[5], t2[1]))
            piv = jnp.where(m8,
                            jnp.where(m4, jnp.where(m2, t2[14], t2[12]), jnp.where(m2, t2[10], t2[8])),
                            jnp.where(m4, jnp.where(m2, t2[6], t2[4]), jnp.where(m2, t2[2], t2[0])))
            rank = (jnp.where(m8, 8.0, 0.0) + jnp.where(m4, 4.0, 0.0) + jnp.where(m2, 2.0, 0.0)
                    + jnp.where(y < piv, 1.0, 0.0))
            r2_parts.append(jnp.where(y < t2[PEER_TOPK - 1], float(PEER_TOPK), rank))
            e2_parts.append(jnp.exp(y - t2[0]))
        c1_out[h] = jnp.concatenate(c1_parts, axis=0)
        coef_out[h] = jnp.concatenate(coef_parts, axis=0)
        r2_out[h] = jnp.concatenate(r2_parts, axis=0).astype(BF16)
        e2_out[h] = jnp.concatenate(e2_parts, axis=0).astype(BF16)
        return carry

    lax.fori_loop(0, PEER_HEADS, body, 0)


def _topk_call(s1t, s2t, *, tl):
    n = s1t.shape[2]
    blk = pl.BlockSpec((PEER_HEADS, N_KEYS, tl), lambda i: (0, 0, i))
    shp = jax.ShapeDtypeStruct((PEER_HEADS, N_KEYS, n), F32)
    shp_b = jax.ShapeDtypeStruct((PEER_HEADS, N_KEYS, n), BF16)
    return pl.pallas_call(
        _topk_kernel,
        grid=(n // tl,),
        in_specs=[blk, blk],
        out_specs=[blk, blk, blk, blk],
        out_shape=[shp_b, shp, shp, shp_b],
        compiler_params=_params("arbitrary"),
        name="topk",
    )(s1t, s2t)


def _dense_kernel(hx2_ref, u0_ref, ux_ref, uy_ref, vtx_ref, vty_ref, vtl_ref, r2_ref, e2_ref, c1_ref, coef_ref,
                  x1_ref, mod_ref, gpost_ref, o_ref, ac_scr, a_scr, wc_scr, w_scr, acc_scr, r2_scr, e2_scr):
    k = pl.program_id(1)
    te = ux_ref.shape[0]
    tm = hx2_ref.shape[0]
    nt = N_KEYS // BF16_ROWS
    groups = te // N_KEYS
    zero = jnp.zeros((), BF16)

    def pre(u_ref):
        a_t = lax.dot_general(u_ref[...], hx2_ref[...], NT_DIMS, preferred_element_type=F32)
        return _gelu_tanh(a_t).astype(BF16)

    def mix(a_ref, w_ref, row0):
        for g in range(groups):
            ks = slice(g * N_KEYS, (g + 1) * N_KEYS)
            r = row0 + g
            for c in range(tm // LANE):
                ls = slice(c * LANE, (c + 1) * LANE)
                gate = jnp.zeros((N_KEYS, LANE), BF16)
                for h in range(PEER_HEADS):
                    cnt = jnp.broadcast_to(c1_ref[h, r:r + 1, ls], (BF16_ROWS, LANE)).astype(BF16)
                    cf = jnp.broadcast_to(coef_ref[h, r:r + 1, ls], (BF16_ROWS, LANE)).astype(BF16)
                    hit = r2_scr[h, :, ls] < jnp.tile(cnt, (nt, 1))
                    gate = gate + jnp.where(hit, e2_scr[h, :, ls] * jnp.tile(cf, (nt, 1)), zero)
                w_ref[ks, ls] = a_ref[ks, ls] * gate

    @pl.when(k == 0)
    def _():
        acc_scr[...] = jnp.zeros_like(acc_scr)
        wc_scr[0] = jnp.zeros(wc_scr.shape[1:], BF16)
        ac_scr[0] = pre(u0_ref)
        for h in range(PEER_HEADS):
            r2_scr[h] = r2_ref[h]
            e2_scr[h] = e2_ref[h]

    p = lax.rem(k, 2)
    q = 1 - p
    mix(ac_scr.at[p], w_scr, 0)
    a_scr[...] = pre(ux_ref)
    acc_scr[...] += _dot(vtx_ref[...], wc_scr[p])
    mix(a_scr, wc_scr.at[q], groups)
    ac_scr[q] = pre(uy_ref)
    acc_scr[...] += _dot(vty_ref[...], w_scr[...])

    @pl.when(k == pl.num_programs(1) - 1)
    def _():
        peer = (acc_scr[...] + _dot(vtl_ref[...], wc_scr[q])).T
        o_ref[...] = x1_ref[...] + mod_ref[0, 5:6, :] * _rms(peer, gpost_ref[...])


def _dense_call(hx2, u_b, vt_b, r2, e2, c1, coef, x1, mod3, gpost2, *, seq, tm, te):
    n = hx2.shape[0]
    ne = u_b.shape[0]
    tiles = seq // tm
    nb = ne // te
    assert nb % 2 == 0
    tok = lambda i, k: (0, 0, i)
    u_blk = lambda fn: pl.BlockSpec((te, D_MODEL), lambda i, k: (fn(k), 0))
    vt_blk = lambda fn: pl.BlockSpec((None, D_MODEL, te), lambda i, k: (fn(k), 0, 0))
    pair_rows = 2 * te // N_KEYS
    return pl.pallas_call(
        _dense_kernel,
        grid=(n // tm, nb // 2),
        in_specs=[pl.BlockSpec((tm, D_MODEL), lambda i, k: (i, 0)),
                  u_blk(lambda k: 0),
                  u_blk(lambda k: 2 * k + 1),
                  u_blk(lambda k: jnp.minimum(2 * k + 2, nb - 1)),
                  vt_blk(lambda k: jnp.maximum(2 * k - 1, 0)),
                  vt_blk(lambda k: 2 * k),
                  vt_blk(lambda k: nb - 1),
                  pl.BlockSpec((PEER_HEADS, N_KEYS, tm), tok),
                  pl.BlockSpec((PEER_HEADS, N_KEYS, tm), tok),
                  pl.BlockSpec((PEER_HEADS, pair_rows, tm), lambda i, k: (0, k, i)),
                  pl.BlockSpec((PEER_HEADS, pair_rows, tm), lambda i, k: (0, k, i)),
                  pl.BlockSpec((tm, D_MODEL), lambda i, k: (i, 0)),
                  pl.BlockSpec((1, N_MOD, D_MODEL), lambda i, k: (i // tiles, 0, 0)),
                  pl.BlockSpec((1, D_MODEL), lambda i, k: (0, 0))],
        out_specs=pl.BlockSpec((tm, D_MODEL), lambda i, k: (i, 0)),
        out_shape=jax.ShapeDtypeStruct((n, D_MODEL), F32),
        scratch_shapes=[pltpu.VMEM((2, te, tm), BF16), pltpu.VMEM((te, tm), BF16),
                        pltpu.VMEM((2, te, tm), BF16), pltpu.VMEM((te, tm), BF16),
                        pltpu.VMEM((D_MODEL, tm), F32),
                        pltpu.VMEM((PEER_HEADS, N_KEYS, tm), BF16), pltpu.VMEM((PEER_HEADS, N_KEYS, tm), BF16)],
        compiler_params=_params("arbitrary", "arbitrary"),
        name="dense",
    )(hx2, u_b, u_b, u_b, vt_b, vt_b, vt_b, r2, e2, c1, coef, x1, mod3, gpost2)


def _rot_cols(w):
    lead = w.shape[:-1]
    w4 = w.reshape(lead + (2, 2, QK_ROPE // 4))
    return jnp.concatenate([-w4[..., 1:2, :], w4[..., 0:1, :]], axis=-2).reshape(w.shape)


def _pad_lanes(w):
    return jnp.concatenate([w, jnp.zeros(w.shape[:-1] + (LANE - w.shape[-1],), w.dtype)], axis=-1)


def _rope_tables(seq, dtype):
    quarter = QK_ROPE // 4
    n_rows = seq // GRID_W
    freqs = ROPE_THETA ** (-jnp.arange(quarter, dtype=jnp.float32) / quarter)
    ang_r = jnp.arange(n_rows).astype(jnp.float32)[:, None] * freqs[None, :]
    ang_c = jnp.arange(GRID_W).astype(jnp.float32)[:, None] * freqs[None, :]

    def table(fn):
        by_row = jnp.broadcast_to(fn(ang_r)[:, None, :], (n_rows, GRID_W, quarter))
        by_col = jnp.broadcast_to(fn(ang_c)[None, :, :], (n_rows, GRID_W, quarter))
        t = jnp.concatenate([by_row, by_row, by_col, by_col], axis=-1).reshape(seq, QK_ROPE)
        return _pad_lanes(t).astype(dtype)

    return table(jnp.cos), table(jnp.sin)


def _tiles(seq):
    tm = 256 if seq % 256 == 0 else CHUNK
    tk = next(cand for cand in (1024, 512, 256, 128) if seq % cand == 0)
    big = 512 if seq % 512 == 0 else tm
    return dict(tm=tm, tq=big, tk=tk, tl=256 if seq % 256 == 0 else LANE, td=big, te=1024)


def kernel(x, c, ctx, c_ctx, w_mod, b_mod, g_pre1, g_post1, g_pre2, g_post2, w_in, g_q, g_kv, w_uq, w_ukv,
           g_sgu, w_s, b_s, w_pa, w_pb, w_o, w_pq, k1, k2, u_exp, v_exp):
    batch, seq, d = x.shape
    ctx_len = ctx.shape[1]
    depth = w_mod.shape[0]
    assert depth == 1 and d == D_MODEL and batch + 1 <= 8
    assert seq % CHUNK == 0 and ctx_len % SUB == 0
    tl = _tiles(seq)
    l = 0

    cvec = jnp.zeros((8, d), F32).at[:batch].set(c).at[batch].set(c_ctx)
    mod = _mod_call(cvec, w_mod[l], b_mod[l][None, :])
    mod3 = mod[:batch + 1].reshape(batch + 1, N_MOD, d)

    wi = w_in[l]
    off_q = 2 * SGU_WIDTH
    off_kv = off_q + Q_RANK
    off_kr = off_kv + KV_RANK
    off_ga = off_kr + QK_ROPE
    w_kr = wi[:, off_kr:off_ga]
    win = jnp.concatenate([wi[:, :off_kr], _pad_lanes(w_kr), _pad_lanes(_rot_cols(w_kr)), wi[:, off_ga:]],
                          axis=1).astype(BF16)
    wq3 = w_uq[l].reshape(Q_RANK, MLA_HEADS, QK_NOPE + QK_ROPE)
    wq_rope = wq3[:, :, QK_NOPE:]
    wq = jnp.concatenate([wq3[:, :, :QK_NOPE].reshape(Q_RANK, -1),
                          _pad_lanes(wq_rope).reshape(Q_RANK, -1),
                          _pad_lanes(_rot_cols(wq_rope)).reshape(Q_RANK, -1)], axis=1).astype(BF16)
    wkv3 = w_ukv[l].reshape(KV_RANK, MLA_HEADS, QK_NOPE + V_DIM)
    wkv = jnp.concatenate([wkv3[:, :, :QK_NOPE].reshape(KV_RANK, -1),
                           wkv3[:, :, QK_NOPE:].reshape(KV_RANK, -1)], axis=1).astype(BF16)
    bs = jnp.repeat(b_s[l].T, SGU_WIDTH // SGU_GROUPS, axis=1)
    cos, sin = _rope_tables(seq, x.dtype)
    row = lambda g: g[l][None, :]

    x2 = x.reshape(batch * seq, d)
    q, k_x, v_x, pa, sgb = _inproj_call(
        x2, mod3, row(g_pre1), win, row(g_q), row(g_kv), wq, wkv, row(g_sgu), w_s[l].astype(BF16), bs,
        w_pa[l].astype(BF16), cos, sin, batch=batch, seq=seq, tm=tl["tm"])
    k_c, v_c = _ctxkv_call(ctx, mod3, row(g_pre1), win, row(g_kv), wkv, batch=batch)

    ob = _attn_call(q, k_x, v_x, k_c, v_c, batch=batch, seq=seq, tq=tl["tq"], tk=tl["tk"])

    x1, hx2, s1t, s2t = _post_call(
        x2, ob, pa, sgb, mod3, w_pb[l].astype(BF16), w_o[l].astype(BF16), row(g_post1), row(g_pre2),
        w_pq[l].astype(BF16), k1[l].astype(BF16), k2[l].astype(BF16), seq=seq, tm=tl["tm"])

    r2, c1, coef, e2 = _topk_call(s1t, s2t, tl=tl["tl"])

    te = tl["te"]
    vt_b = v_exp[l].astype(BF16).reshape(-1, te, d).transpose(0, 2, 1)
    out = _dense_call(hx2, u_exp[l].astype(BF16), vt_b, r2, e2, c1, coef, x1, mod3,
                      row(g_post2), seq=seq, tm=tl["td"], te=te)
    return out.reshape(batch, seq, d)
```

```python
import functools

import jax
import jax.numpy as jnp
from jax import lax
from jax.experimental import pallas as pl
from jax.experimental.pallas import tpu as pltpu

F32 = jnp.float32
BF16 = jnp.bfloat16

D_MODEL = 1024
N_MOD = 6
EPS = 1e-6
GRID_W = 64
SGU_WIDTH = 1024
SGU_GROUPS = 8
CHUNK = 128
MLA_HEADS = 8
QK_NOPE = 128
QK_ROPE = 64
V_DIM = 128
Q_RANK = 256
KV_RANK = 128
ROPE_THETA = 10000.0
PEER_HEADS = 8
N_KEYS = 128
PEER_HALF = 128
PEER_TOPK = 16

LOG2E = 1.4426950408889634
LANE = 128
BF16_ROWS = 16
HEAD_PAD = 2 * LANE
VMEM_LIMIT = 56 * 1024 * 1024

C_U = 0
C_V = C_U + SGU_WIDTH
C_Q = C_V + SGU_WIDTH
C_KV = C_Q + Q_RANK
C_KR = C_KV + KV_RANK
C_KT = C_KR + LANE
C_GA = C_KT + LANE
C_GB = C_GA + D_MODEL
C_END = C_GB + D_MODEL

NT_DIMS = (((1,), (1,)), ((), ()))

def _rms(x, g):
    return x * lax.rsqrt(jnp.mean(x * x, axis=-1, keepdims=True) + EPS) * g


def _dot(a, b):
    return jnp.dot(a, b, preferred_element_type=F32)


def _gelu_tanh(x):
    c0 = 0.7978845608028654
    c1 = c0 * 0.044715
    half = 0.5 * x
    return half * jnp.tanh(x * (c0 + c1 * (x * x))) + half


def _params(*sem):
    return pltpu.CompilerParams(dimension_semantics=sem, vmem_limit_bytes=VMEM_LIMIT)


def _mod_kernel(c_ref, w_ref, b_ref, o_ref):
    c = c_ref[...]
    a = (c * jax.nn.sigmoid(c)).astype(BF16)
    o_ref[...] = _dot(a, w_ref[...].astype(BF16)) + b_ref[...]


def _mod_call(cvec, w_mod, b_mod):
    n = w_mod.shape[1]
    bn = n // 4
    return pl.pallas_call(
        _mod_kernel,
        grid=(n // bn,),
        in_specs=[pl.BlockSpec((8, D_MODEL), lambda j: (0, 0)),
                  pl.BlockSpec((D_MODEL, bn), lambda j: (0, j)),
                  pl.BlockSpec((1, bn), lambda j: (0, j))],
        out_specs=pl.BlockSpec((8, bn), lambda j: (0, j)),
        out_shape=jax.ShapeDtypeStruct((8, n), F32),
        compiler_params=_params("arbitrary"),
        name="mod",
    )(cvec, w_mod, b_mod)


def _kv_path(hb, win_ref, gkv_ref, wkv_ref, cos, sin, k_out, v_out):
    rows = hb.shape[0]
    ckv = _dot(hb, win_ref[:, C_KV:C_KR])
    ckvn = _rms(ckv, gkv_ref[...]).astype(BF16)
    kn = _dot(ckvn, wkv_ref[:, :MLA_HEADS * QK_NOPE])
    vv = _dot(ckvn, wkv_ref[:, MLA_HEADS * QK_NOPE:])
    kr = _dot(hb, win_ref[:, C_KR:C_KT])
    if cos is not None:
        kr = kr * cos + _dot(hb, win_ref[:, C_KT:C_GA]) * sin
    krb = kr.astype(BF16)
    ones = jnp.ones((rows, LANE), BF16)
    for h in range(MLA_HEADS):
        k_out[0, :, h * HEAD_PAD:h * HEAD_PAD + LANE] = kn[:, h * LANE:(h + 1) * LANE].astype(BF16)
        k_out[0, :, h * HEAD_PAD + LANE:(h + 1) * HEAD_PAD] = krb
        v_out[0, :, h * HEAD_PAD:h * HEAD_PAD + LANE] = vv[:, h * LANE:(h + 1) * LANE].astype(BF16)
        v_out[0, :, h * HEAD_PAD + LANE:(h + 1) * HEAD_PAD] = ones


def _inproj_kernel(x_ref, mod_ref, gpre_ref, win_ref, gq_ref, gkv_ref, wq_ref, wkv_ref, gsgu_ref,
                   ws_ref, bs_ref, wpa_ref, cos_ref, sin_ref,
                   q_out, k_out, v_out, pa_out, sgb_out, *, q_scale):
    tm = x_ref.shape[0]
    x = x_ref[...]
    hb = (_rms(x, gpre_ref[...]) * (1.0 + mod_ref[0, 1:2, :]) + mod_ref[0, 0:1, :]).astype(BF16)
    cos = cos_ref[...]
    sin = sin_ref[...]

    gu = jax.nn.gelu(_dot(hb, win_ref[:, C_U:C_V]))
    vn = _rms(jax.nn.gelu(_dot(hb, win_ref[:, C_V:C_Q])), gsgu_ref[...]).astype(BF16)
    chunks = []
    for ci in range(tm // CHUNK):
        cols = [_dot(ws_ref[g], vn[ci * CHUNK:(ci + 1) * CHUNK, g * LANE:(g + 1) * LANE])
                for g in range(SGU_GROUPS)]
        chunks.append(jnp.concatenate(cols, axis=1) + bs_ref[...])
    oa = (gu * jnp.concatenate(chunks, axis=0)).astype(BF16)
    pa_out[...] = (jax.nn.sigmoid(_dot(hb, win_ref[:, C_GA:C_GB])) * _dot(oa, wpa_ref[...])).astype(BF16)
    sgb_out[...] = jax.nn.sigmoid(_dot(hb, win_ref[:, C_GB:C_END])).astype(BF16)

    cqn = _rms(_dot(hb, win_ref[:, C_Q:C_KV]), gq_ref[...]).astype(BF16)
    nw = MLA_HEADS * LANE
    qn = _dot(cqn, wq_ref[:, :nw])
    qr = _dot(cqn, wq_ref[:, nw:2 * nw])
    qt = _dot(cqn, wq_ref[:, 2 * nw:])
    for h in range(MLA_HEADS):
        sl = slice(h * LANE, (h + 1) * LANE)
        q_out[:, h * HEAD_PAD:h * HEAD_PAD + LANE] = (qn[:, sl] * q_scale).astype(BF16)
        q_out[:, h * HEAD_PAD + LANE:(h + 1) * HEAD_PAD] = (
            (qr[:, sl] * cos + qt[:, sl] * sin) * q_scale).astype(BF16)

    _kv_path(hb, win_ref, gkv_ref, wkv_ref, cos, sin, k_out, v_out)


def _ctxkv_kernel(c_ref, mod_ref, gpre_ref, win_ref, gkv_ref, wkv_ref, k_out, v_out):
    hb =(_rms(c_ref[0], gpre_ref[...]) * (1.0 + mod_ref[0, 1:2, :]) + mod_ref[0, 0:1, :]).astype(BF16)
    _kv_path(hb, win_ref, gkv_ref, wkv_ref, None, None, k_out, v_out)


def _const(shape):
    nd = len(shape)
    return pl.BlockSpec(shape, lambda *_: (0,) * nd)


def _inproj_call(x2, mod3, gpre, win, gq, gkv, wq, wkv, gsgu, ws, bs, wpa, cos, sin, *, batch, seq, tm):
    n = batch * seq
    tiles = seq // tm
    hw = MLA_HEADS * HEAD_PAD
    row = lambda i: (i, 0)
    kern = functools.partial(_inproj_kernel, q_scale=float((QK_NOPE + QK_ROPE) ** -0.5 * LOG2E))
    return pl.pallas_call(
        kern,
        grid=(n // tm,),
        in_specs=[pl.BlockSpec((tm, D_MODEL), row),
                  pl.BlockSpec((1, N_MOD, D_MODEL), lambda i: (i // tiles, 0, 0)),
                  _const(gpre.shape), _const(win.shape), _const(gq.shape), _const(gkv.shape),
                  _const(wq.shape), _const(wkv.shape), _const(gsgu.shape), _const(ws.shape),
                  _const(bs.shape), _const(wpa.shape),
                  pl.BlockSpec((tm, LANE), lambda i: (i % tiles, 0)),
                  pl.BlockSpec((tm, LANE), lambda i: (i % tiles, 0))],
        out_specs=[pl.BlockSpec((tm, hw), row),
                   pl.BlockSpec((1, tm, hw), lambda i: (i // tiles, i % tiles, 0)),
                   pl.BlockSpec((1, tm, hw), lambda i: (i // tiles, i % tiles, 0)),
                   pl.BlockSpec((tm, D_MODEL), row),
                   pl.BlockSpec((tm, D_MODEL), row)],
        out_shape=[jax.ShapeDtypeStruct((n, hw), BF16),
                   jax.ShapeDtypeStruct((batch, seq, hw), BF16),
                   jax.ShapeDtypeStruct((batch, seq, hw), BF16),
                   jax.ShapeDtypeStruct((n, D_MODEL), BF16),
                   jax.ShapeDtypeStruct((n, D_MODEL), BF16)],
        compiler_params=_params("arbitrary"),
        name="inproj",
    )(x2, mod3, gpre, win, gq, gkv, wq, wkv, gsgu, ws, bs, wpa, cos, sin)


def _ctxkv_call(ctx, mod3, gpre, win, gkv, wkv, *, batch):
    ctx_len = ctx.shape[1]
    hw = MLA_HEADS * HEAD_PAD
    blk = pl.BlockSpec((1, ctx_len, hw), lambda b: (b, 0, 0))
    shp = jax.ShapeDtypeStruct((batch, ctx_len, hw), BF16)
    return pl.pallas_call(
        _ctxkv_kernel,
        grid=(batch,),
        in_specs=[pl.BlockSpec((1, ctx_len, D_MODEL), lambda b: (b, 0, 0)),
                  pl.BlockSpec((1, N_MOD, D_MODEL), lambda b: (batch, 0, 0)),
                  _const(gpre.shape), _const(win.shape), _const(gkv.shape), _const(wkv.shape)],
        out_specs=[blk, blk],
        out_shape=[shp, shp],
        compiler_params=_params("arbitrary"),
        name="ctxkv",
    )(ctx, mod3, gpre, win, gkv, wkv)


def _attn_kernel(q_ref, kx_ref, vx_ref, kc_ref, vc_ref, o_ref, *, tk):
    q = q_ref[...]
    tq = q.shape[0]
    m = jnp.full((tq, 1), -jnp.inf, F32)
    acc = jnp.zeros((tq, HEAD_PAD), F32)
    chunks = [(kx_ref, vx_ref, slice(c * tk, (c + 1) * tk)) for c in range(kx_ref.shape[1] // tk)]
    chunks.append((kc_ref, vc_ref, slice(0, kc_ref.shape[1])))
    for k_ref, v_ref, rows in chunks:
        s = lax.dot_general(q, k_ref[0, rows, :], NT_DIMS, preferred_element_type=F32)
        m_new = jnp.maximum(m, jnp.max(s, axis=1, keepdims=True))
        p = jnp.exp2(s - m_new).astype(BF16)
        acc = acc * jnp.exp2(m - m_new) + _dot(p, v_ref[0, rows, :])
        m = m_new
    o_ref[...] = (acc[:, :V_DIM] / acc[:, V_DIM:]).astype(BF16)


def _attn_call(q, k_x, v_x, k_c, v_c, *, batch, seq, tq, tk):
    ctx_len = k_c.shape[1]
    tiles = seq // tq
    per_head = lambda rows: pl.BlockSpec((1, rows, HEAD_PAD), lambda b, h, i: (b, 0, h))
    return pl.pallas_call(
        functools.partial(_attn_kernel, tk=tk),
        grid=(batch, MLA_HEADS, tiles),
        in_specs=[pl.BlockSpec((tq, HEAD_PAD), lambda b, h, i: (b * tiles + i, h)),
                  per_head(seq), per_head(seq), per_head(ctx_len), per_head(ctx_len)],
        out_specs=pl.BlockSpec((tq, V_DIM), lambda b, h, i: (b * tiles + i, h)),
        out_shape=jax.ShapeDtypeStruct((batch * seq, MLA_HEADS * V_DIM), BF16),
        compiler_params=_params("arbitrary", "arbitrary", "arbitrary"),
        name="attn",
    )(q, k_x, v_x, k_c, v_c)


def _post_kernel(x_ref, ob_ref, pa_ref, sgb_ref, mod_ref, wpb_ref, wo_ref, gpost_ref, gpre2_ref,
                 wpq_ref, k1_ref, k2_ref, x1_out, hx2_out, s1_out, s2_out):
    t = _dot(ob_ref[...], wpb_ref[...])
    mix = (pa_ref[...].astype(F32) + sgb_ref[...].astype(F32) * t).astype(BF16)
    y = _dot(mix, wo_ref[...])
    x1 = x_ref[...] + mod_ref[0, 2:3, :] * _rms(y, gpost_ref[...])
    x1_out[...] = x1
    hx2 = (_rms(x1, gpre2_ref[...]) * (1.0 + mod_ref[0, 4:5, :]) + mod_ref[0, 3:4, :]).astype(BF16)
    hx2_out[...] = hx2
    qp = _dot(hx2, wpq_ref[...]).astype(BF16)
    k1 = k1_ref[...]
    k2 = k2_ref[...]
    for h in range(PEER_HEADS):
        base = h * 2 * PEER_HALF
        s1_out[h] = lax.dot_general(k1, qp[:, base:base + PEER_HALF], NT_DIMS, preferred_element_type=F32)
        s2_out[h] = lax.dot_general(k2, qp[:, base + PEER_HALF:base + 2 * PEER_HALF], NT_DIMS,
                                    preferred_element_type=F32)


def _post_call(x2, ob, pa, sgb, mod3, wpb, wo, gpost, gpre2, wpq, k1, k2, *, seq, tm):
    n = x2.shape[0]
    tiles = seq // tm
    row = lambda i: (i, 0)
    tok = lambda i: (0, 0, i)
    return pl.pallas_call(
        _post_kernel,
        grid=(n // tm,),
        in_specs=[pl.BlockSpec((tm, D_MODEL), row), pl.BlockSpec((tm, D_MODEL), row),
                  pl.BlockSpec((tm, D_MODEL), row), pl.BlockSpec((tm, D_MODEL), row),
                  pl.BlockSpec((1, N_MOD, D_MODEL), lambda i: (i // tiles, 0, 0)),
                  _const(wpb.shape), _const(wo.shape), _const(gpost.shape), _const(gpre2.shape),
                  _const(wpq.shape), _const(k1.shape), _const(k2.shape)],
        out_specs=[pl.BlockSpec((tm, D_MODEL), row), pl.BlockSpec((tm, D_MODEL), row),
                   pl.BlockSpec((PEER_HEADS, N_KEYS, tm), tok), pl.BlockSpec((PEER_HEADS, N_KEYS, tm), tok)],
        out_shape=[jax.ShapeDtypeStruct((n, D_MODEL), F32), jax.ShapeDtypeStruct((n, D_MODEL), BF16),
                   jax.ShapeDtypeStruct((PEER_HEADS, N_KEYS, n), F32),
                   jax.ShapeDtypeStruct((PEER_HEADS, N_KEYS, n), F32)],
        compiler_params=_params("arbitrary"),
        name="post",
    )(x2, ob, pa, sgb, mod3, wpb, wo, gpost, gpre2, wpq, k1, k2)


SUB = 8


def _sort_pairs(n):
    pairs = []

    def merge(lo, hi, r):
        step = r * 2
        if step < hi - lo:
            merge(lo, hi, step)
            merge(lo + r, hi, step)
            pairs.extend((i, i + r) for i in range(lo + r, hi - r, step))
        else:
            pairs.append((lo, lo + r))

    def sort(lo, hi):
        if hi - lo >= 1:
            mid = lo + (hi - lo) // 2
            sort(lo, mid)
            sort(mid + 1, hi)
            merge(lo, hi, 1)

    sort(0, n - 1)
    return pairs


def _cmpx(v, i, j):
    a, b = v[i], v[j]
    if b is None:
        return
    if a is None:
        v[i], v[j] = b, None
        return
    v[i], v[j] = jnp.maximum(a, b), jnp.minimum(a, b)


def _top16_sorted(slabs):
    v = list(slabs) + [None] * (PEER_TOPK - len(slabs))
    for i, j in _sort_pairs(PEER_TOPK):
        _cmpx(v, i, j)
    for shift in (SUB // 2, SUB // 4, SUB // 8):
        rolled = [None if x is None else pltpu.roll(x, shift, axis=0) for x in v]
        merged = []
        for i in range(PEER_TOPK):
            a, b = v[i], rolled[PEER_TOPK - 1 - i]
            merged.append(b if a is None else (a if b is None else jnp.maximum(a, b)))
        v = merged
        d = PEER_TOPK // 2
        while d >= 1:
            for i in range(PEER_TOPK):
                if i & d == 0:
                    _cmpx(v, i, i + d)
            d //= 2
    return v


def _topk_kernel(s1_ref, s2_ref, r2_out, c1_out, coef_out, e2_out):
    nslab = N_KEYS // SUB
    tl = s1_ref.shape[2]
    row = lax.broadcasted_iota(jnp.int32, (SUB, tl), 0)
    row_is = [row == b for b in range(SUB)]

    def spread(vals):
        out = vals[SUB - 1]
        for b in range(SUB - 2, -1, -1):
            out = jnp.where(row_is[b], vals[b], out)
        return out

    def body(h, carry):
        s1 = [s1_ref[h, j * SUB:(j + 1) * SUB, :] for j in range(nslab)]
        s2 = [s2_ref[h, j * SUB:(j + 1) * SUB, :] for j in range(nslab)]
        t1 = _top16_sorted(s1)
        t2 = _top16_sorted(s2)

        t2_lo, t2_hi, t1_hi = spread(t2[:SUB]), spread(t2[SUB:]), spread(t1[SUB:])
        cand = [t1[0] + t2_lo, t1[0] + t2_hi] + [t1[a] + t2_lo for a in range(1, SUB)] + [t1_hi + t2[0]]
        best = _top16_sorted(cand)
        tau = best[PEER_TOPK - 1]
        z = jnp.ones_like(tau)
        for k in range(1, PEER_TOPK):
            z = z + jnp.exp(best[k] - best[0])

        def count(base, bs):
            return sum(jnp.where(base + t2[b] >= tau, 1.0, 0.0) for b in bs)

        extra = [count(t1[0], range(4, 16)), count(t1[1], range(4, 8)), count(t1[2], range(4, 5))]
        inv_z = 1.0 / z
        c1_parts, coef_parts, r2_parts, e2_parts = [], [], [], []
        for j in range(nslab):
            x = s1[j]
            c = count(x, range(4))
            for a in range(3):
                c = c + jnp.where(x == t1[a], extra[a], 0.0)
            c1_parts.append(c)
            coef_parts.append(jnp.exp(x - t1[0]) * inv_z)
            y = s2[j]
            m8 = y < t2[7]
            m4 = y < jnp.where(m8, t2[11], t2[3])
            m2 = y < jnp.where(m8, jnp.where(m4, t2[13], t2[9]), jnp.where(m4, t2[5], t2[1]))
            piv = jnp.where(m8,
                            jnp.where(m4, jnp.where(m2, t2[14], t2[12]), jnp.where(m2, t2[10], t2[8])),
                            jnp.where(m4, jnp.where(m2, t2[6], t2[4]), jnp.where(m2, t2[2], t2[0])))
            rank = (jnp.where(m8, 8.0, 0.0) + jnp.where(m4, 4.0, 0.0) + jnp.where(m2, 2.0, 0.0)
                    + jnp.where(y < piv, 1.0, 0.0))
            r2_parts.append(jnp.where(y < t2[PEER_TOPK - 1], float(PEER_TOPK), rank))
            e2_parts.append(jnp.exp(y - t2[0]))
        c1_out[h] = jnp.concatenate(c1_parts, axis=0)
        coef_out[h] = jnp.concatenate(coef_parts, axis=0)
        r2_out[h] = jnp.concatenate(r2_parts, axis=0).astype(BF16)
        e2_out[h] = jnp.concatenate(e2_parts, axis=0).astype(BF16)
        return carry

    lax.fori_loop(0, PEER_HEADS, body, 0)


def _topk_call(s1t, s2t, *, tl):
    n = s1t.shape[2]
    blk = pl.BlockSpec((PEER_HEADS, N_KEYS, tl), lambda i: (0, 0, i))
    shp = jax.ShapeDtypeStruct((PEER_HEADS, N_KEYS, n), F32)
    shp_b = jax.ShapeDtypeStruct((PEER_HEADS, N_KEYS, n), BF16)
    return pl.pallas_call(
        _topk_kernel,
        grid=(n // tl,),
        in_specs=[blk, blk],
        out_specs=[blk, blk, blk, blk],
        out_shape=[shp_b, shp, shp, shp_b],
        compiler_params=_params("arbitrary"),
        name="topk",
    )(s1t, s2t)


def _dense_kernel(hx2_ref, u0_ref, ux_ref, uy_ref, vtx_ref, vty_ref, vtl_ref, r2_ref, e2_ref, c1_ref, coef_ref,
                  x1_ref, mod_ref, gpost_ref, o_ref, ac_scr, a_scr, wc_scr, w_scr, acc_scr, r2_scr, e2_scr):
    k = pl.program_id(1)
    te = ux_ref.shape[0]
    tm = hx2_ref.shape[0]
    nt = N_KEYS // BF16_ROWS
    groups = te // N_KEYS
    zero = jnp.zeros((), BF16)

    def pre(u_ref):
        a_t = lax.dot_general(u_ref[...], hx2_ref[...], NT_DIMS, preferred_element_type=F32)
        return _gelu_tanh(a_t).astype(BF16)

    def mix(a_ref, w_ref, row0):
        for g in range(groups):
            ks = slice(g * N_KEYS, (g + 1) * N_KEYS)
            r = row0 + g
            for c in range(tm // LANE):
                ls = slice(c * LANE, (c + 1) * LANE)
                gate = jnp.zeros((N_KEYS, LANE), BF16)
                for h in range(PEER_HEADS):
                    cnt = jnp.broadcast_to(c1_ref[h, r:r + 1, ls], (BF16_ROWS, LANE)).astype(BF16)
                    cf = jnp.broadcast_to(coef_ref[h, r:r + 1, ls], (BF16_ROWS, LANE)).astype(BF16)
                    hit = r2_scr[h, :, ls] < jnp.tile(cnt, (nt, 1))
                    gate = gate + jnp.where(hit, e2_scr[h, :, ls] * jnp.tile(cf, (nt, 1)), zero)
                w_ref[ks, ls] = a_ref[ks, ls] * gate

    @pl.when(k == 0)
    def _():
        acc_scr[...] = jnp.zeros_like(acc_scr)
        wc_scr[0] = jnp.zeros(wc_scr.shape[1:], BF16)
        ac_scr[0] = pre(u0_ref)
        for h in range(PEER_HEADS):
            r2_scr[h] = r2_ref[h]
            e2_scr[h] = e2_ref[h]

    p = lax.rem(k, 2)
    q = 1 - p
    mix(ac_scr.at[p], w_scr, 0)
    a_scr[...] = pre(ux_ref)
    acc_scr[...] += _dot(vtx_ref[...], wc_scr[p])
    mix(a_scr, wc_scr.at[q], groups)
    ac_scr[q] = pre(uy_ref)
    acc_scr[...] += _dot(vty_ref[...], w_scr[...])

    @pl.when(k == pl.num_programs(1) - 1)
    def _():
        peer = (acc_scr[...] + _dot(vtl_ref[...], wc_scr[q])).T
        o_ref[...] = x1_ref[...] + mod_ref[0, 5:6, :] * _rms(peer, gpost_ref[...])


def _dense_call(hx2, u_b, vt_b, r2, e2, c1, coef, x1, mod3, gpost2, *, seq, tm, te):
    n = hx2.shape[0]
    ne = u_b.shape[0]
    tiles = seq // tm
    nb = ne // te
    assert nb % 2 == 0
    tok = lambda i, k: (0, 0, i)
    u_blk = lambda fn: pl.BlockSpec((te, D_MODEL), lambda i, k: (fn(k), 0))
    vt_blk = lambda fn: pl.BlockSpec((None, D_MODEL, te), lambda i, k: (fn(k), 0, 0))
    pair_rows = 2 * te // N_KEYS
    return pl.pallas_call(
        _dense_kernel,
        grid=(n // tm, nb // 2),
        in_specs=[pl.BlockSpec((tm, D_MODEL), lambda i, k: (i, 0)),
                  u_blk(lambda k: 0),
                  u_blk(lambda k: 2 * k + 1),
                  u_blk(lambda k: jnp.minimum(2 * k + 2, nb - 1)),
                  vt_blk(lambda k: jnp.maximum(2 * k - 1, 0)),
                  vt_blk(lambda k: 2 * k),
                  vt_blk(lambda k: nb - 1),
                  pl.BlockSpec((PEER_HEADS, N_KEYS, tm), tok),
                  pl.BlockSpec((PEER_HEADS, N_KEYS, tm), tok),
                  pl.BlockSpec((PEER_HEADS, pair_rows, tm), lambda i, k: (0, k, i)),
                  pl.BlockSpec((PEER_HEADS, pair_rows, tm), lambda i, k: (0, k, i)),
                  pl.BlockSpec((tm, D_MODEL), lambda i, k: (i, 0)),
                  pl.BlockSpec((1, N_MOD, D_MODEL), lambda i, k: (i // tiles, 0, 0)),
                  pl.BlockSpec((1, D_MODEL), lambda i, k: (0, 0))],
        out_specs=pl.BlockSpec((tm, D_MODEL), lambda i, k: (i, 0)),
        out_shape=jax.ShapeDtypeStruct((n, D_MODEL), F32),
        scratch_shapes=[pltpu.VMEM((2, te, tm), BF16), pltpu.VMEM((te, tm), BF16),
                        pltpu.VMEM((2, te, tm), BF16), pltpu.VMEM((te, tm), BF16),
                        pltpu.VMEM((D_MODEL, tm), F32),
                        pltpu.VMEM((PEER_HEADS, N_KEYS, tm), BF16), pltpu.VMEM((PEER_HEADS, N_KEYS, tm), BF16)],
        compiler_params=_params("arbitrary", "arbitrary"),
        name="dense",
    )(hx2, u_b, u_b, u_b, vt_b, vt_b, vt_b, r2, e2, c1, coef, x1, mod3, gpost2)


def _rot_cols(w):
    lead = w.shape[:-1]
    w4 = w.reshape(lead + (2, 2, QK_ROPE // 4))
    return jnp.concatenate([-w4[..., 1:2, :], w4[..., 0:1, :]], axis=-2).reshape(w.shape)


def _pad_lanes(w):
    return jnp.concatenate([w, jnp.zeros(w.shape[:-1] + (LANE - w.shape[-1],), w.dtype)], axis=-1)


def _rope_tables(seq, dtype):
    quarter = QK_ROPE // 4
    n_rows = seq // GRID_W
    freqs = ROPE_THETA ** (-jnp.arange(quarter, dtype=jnp.float32) / quarter)
    ang_r = jnp.arange(n_rows).astype(jnp.float32)[:, None] * freqs[None, :]
    ang_c = jnp.arange(GRID_W).astype(jnp.float32)[:, None] * freqs[None, :]

    def table(fn):
        by_row = jnp.broadcast_to(fn(ang_r)[:, None, :], (n_rows, GRID_W, quarter))
        by_col = jnp.broadcast_to(fn(ang_c)[None, :, :], (n_rows, GRID_W, quarter))
        t = jnp.concatenate([by_row, by_row, by_col, by_col], axis=-1).reshape(seq, QK_ROPE)
        return _pad_lanes(t).astype(dtype)

    return table(jnp.cos), table(jnp.sin)


def _tiles(seq):
    tm = 256 if seq % 256 == 0 else CHUNK
    tk = next(cand for cand in (1024, 512, 256, 128) if seq % cand == 0)
    big = 512 if seq % 512 == 0 else tm
    tq = 1024 if seq % 1024 == 0 else big
    return dict(tm=big, tq=tq, tk=tk, tl=256 if seq % 256 == 0 else LANE, td=big, te=1024)


def kernel(x, c, ctx, c_ctx, w_mod, b_mod, g_pre1, g_post1, g_pre2, g_post2, w_in, g_q, g_kv, w_uq, w_ukv,
           g_sgu, w_s, b_s, w_pa, w_pb, w_o, w_pq, k1, k2, u_exp, v_exp):
    batch, seq, d = x.shape
    ctx_len = ctx.shape[1]
    depth = w_mod.shape[0]
    assert depth == 1 and d == D_MODEL and batch + 1 <= 8
    assert seq % CHUNK == 0 and ctx_len % SUB == 0
    tl = _tiles(seq)
    l = 0

    cvec = jnp.zeros((8, d), F32).at[:batch].set(c).at[batch].set(c_ctx)
    mod = _mod_call(cvec, w_mod[l], b_mod[l][None, :])
    mod3 = mod[:batch + 1].reshape(batch + 1, N_MOD, d)

    wi = w_in[l]
    off_q = 2 * SGU_WIDTH
    off_kv = off_q + Q_RANK
    off_kr = off_kv + KV_RANK
    off_ga = off_kr + QK_ROPE
    w_kr = wi[:, off_kr:off_ga]
    win = jnp.concatenate([wi[:, :off_kr], _pad_lanes(w_kr), _pad_lanes(_rot_cols(w_kr)), wi[:, off_ga:]],
                          axis=1).astype(BF16)
    wq3 = w_uq[l].reshape(Q_RANK, MLA_HEADS, QK_NOPE + QK_ROPE)
    wq_rope = wq3[:, :, QK_NOPE:]
    wq = jnp.concatenate([wq3[:, :, :QK_NOPE].reshape(Q_RANK, -1),
                          _pad_lanes(wq_rope).reshape(Q_RANK, -1),
                          _pad_lanes(_rot_cols(wq_rope)).reshape(Q_RANK, -1)], axis=1).astype(BF16)
    wkv3 = w_ukv[l].reshape(KV_RANK, MLA_HEADS, QK_NOPE + V_DIM)
    wkv = jnp.concatenate([wkv3[:, :, :QK_NOPE].reshape(KV_RANK, -1),
                           wkv3[:, :, QK_NOPE:].reshape(KV_RANK, -1)], axis=1).astype(BF16)
    bs = jnp.repeat(b_s[l].T, SGU_WIDTH // SGU_GROUPS, axis=1)
    cos, sin = _rope_tables(seq, x.dtype)
    row = lambda g: g[l][None, :]

    x2 = x.reshape(batch * seq, d)
    q, k_x, v_x, pa, sgb = _inproj_call(
        x2, mod3, row(g_pre1), win, row(g_q), row(g_kv), wq, wkv, row(g_sgu), w_s[l].astype(BF16), bs,
        w_pa[l].astype(BF16), cos, sin, batch=batch, seq=seq, tm=tl["tm"])
    k_c, v_c = _ctxkv_call(ctx, mod3, row(g_pre1), win, row(g_kv), wkv, batch=batch)

    ob = _attn_call(q, k_x, v_x, k_c, v_c, batch=batch, seq=seq, tq=tl["tq"], tk=tl["tk"])

    x1, hx2, s1t, s2t = _post_call(
        x2, ob, pa, sgb, mod3, w_pb[l].astype(BF16), w_o[l].astype(BF16), row(g_post1), row(g_pre2),
        w_pq[l].astype(BF16), k1[l].astype(BF16), k2[l].astype(BF16), seq=seq, tm=tl["tm"])

    r2, c1, coef, e2 = _topk_call(s1t, s2t, tl=tl["tl"])

    te = tl["te"]
    vt_b = v_exp[l].astype(BF16).reshape(-1, te, d).transpose(0, 2, 1)
    out = _dense_call(hx2, u_exp[l].astype(BF16), vt_b, r2, e2, c1, coef, x1, mod3,
                      row(g_post2), seq=seq, tm=tl["td"], te=te)
    return out.reshape(batch, seq, d)
```

```python
import functools

import jax
import jax.numpy as jnp
from jax import lax
from jax.experimental import pallas as pl
from jax.experimental.pallas import tpu as pltpu

F32 = jnp.float32
BF16 = jnp.bfloat16

D_MODEL = 1024
N_MOD = 6
EPS = 1e-6
GRID_W = 64
SGU_WIDTH = 1024
SGU_GROUPS = 8
CHUNK = 128
MLA_HEADS = 8
QK_NOPE = 128
QK_ROPE = 64
V_DIM = 128
Q_RANK = 256
KV_RANK = 128
ROPE_THETA = 10000.0
PEER_HEADS = 8
N_KEYS = 128
PEER_HALF = 128
PEER_TOPK = 16

LOG2E = 1.4426950408889634
LANE = 128
BF16_ROWS = 16
HEAD_PAD = 2 * LANE
VMEM_LIMIT = 56 * 1024 * 1024

C_U = 0
C_V = C_U + SGU_WIDTH
C_Q = C_V + SGU_WIDTH
C_KV = C_Q + Q_RANK
C_KR = C_KV + KV_RANK
C_KT = C_KR + LANE
C_GA = C_KT + LANE
C_GB = C_GA + D_MODEL
C_END = C_GB + D_MODEL

NT_DIMS = (((1,), (1,)), ((), ()))

def _rms(x, g):
    return x * lax.rsqrt(jnp.mean(x * x, axis=-1, keepdims=True) + EPS) * g


def _dot(a, b):
    return jnp.dot(a, b, preferred_element_type=F32)


def _gelu_tanh(x):
    c0 = 0.7978845608028654
    c1 = c0 * 0.044715
    half = 0.5 * x
    return half * jnp.tanh(x * (c0 + c1 * (x * x))) + half


def _params(*sem):
    return pltpu.CompilerParams(dimension_semantics=sem, vmem_limit_bytes=VMEM_LIMIT)


def _mod_kernel(c_ref, w_ref, b_ref, o_ref):
    c = c_ref[...]
    a = (c * jax.nn.sigmoid(c)).astype(BF16)
    o_ref[...] = _dot(a, w_ref[...].astype(BF16)) + b_ref[...]


def _mod_call(cvec, w_mod, b_mod):
    n = w_mod.shape[1]
    bn = n // 4
    return pl.pallas_call(
        _mod_kernel,
        grid=(n // bn,),
        in_specs=[pl.BlockSpec((8, D_MODEL), lambda j: (0, 0)),
                  pl.BlockSpec((D_MODEL, bn), lambda j: (0, j)),
                  pl.BlockSpec((1, bn), lambda j: (0, j))],
        out_specs=pl.BlockSpec((8, bn), lambda j: (0, j)),
        out_shape=jax.ShapeDtypeStruct((8, n), F32),
        compiler_params=_params("arbitrary"),
        name="mod",
    )(cvec, w_mod, b_mod)


def _kv_path(hb, win_ref, gkv_ref, wkv_ref, cos, sin, k_out, v_out):
    rows = hb.shape[0]
    ckv = _dot(hb, win_ref[:, C_KV:C_KR])
    ckvn = _rms(ckv, gkv_ref[...]).astype(BF16)
    kn = _dot(ckvn, wkv_ref[:, :MLA_HEADS * QK_NOPE])
    vv = _dot(ckvn, wkv_ref[:, MLA_HEADS * QK_NOPE:])
    kr = _dot(hb, win_ref[:, C_KR:C_KT])
    if cos is not None:
        kr = kr * cos + _dot(hb, win_ref[:, C_KT:C_GA]) * sin
    krb = kr.astype(BF16)
    ones = jnp.ones((rows, LANE), BF16)
    for h in range(MLA_HEADS):
        k_out[0, :, h * HEAD_PAD:h * HEAD_PAD + LANE] = kn[:, h * LANE:(h + 1) * LANE].astype(BF16)
        k_out[0, :, h * HEAD_PAD + LANE:(h + 1) * HEAD_PAD] = krb
        v_out[0, :, h * HEAD_PAD:h * HEAD_PAD + LANE] = vv[:, h * LANE:(h + 1) * LANE].astype(BF16)
        v_out[0, :, h * HEAD_PAD + LANE:(h + 1) * HEAD_PAD] = ones


def _inproj_kernel(x_ref, mod_ref, gpre_ref, win_ref, gq_ref, gkv_ref, wq_ref, wkv_ref, gsgu_ref,
                   ws_ref, bs_ref, wpa_ref, cos_ref, sin_ref,
                   q_out, k_out, v_out, pa_out, sgb_out, *, q_scale):
    tm = x_ref.shape[0]
    x = x_ref[...]
    hb = (_rms(x, gpre_ref[...]) * (1.0 + mod_ref[0, 1:2, :]) + mod_ref[0, 0:1, :]).astype(BF16)
    cos = cos_ref[...]
    sin = sin_ref[...]

    gu = jax.nn.gelu(_dot(hb, win_ref[:, C_U:C_V]))
    vn = _rms(jax.nn.gelu(_dot(hb, win_ref[:, C_V:C_Q])), gsgu_ref[...]).astype(BF16)
    chunks = []
    for ci in range(tm // CHUNK):
        cols = [_dot(ws_ref[g], vn[ci * CHUNK:(ci + 1) * CHUNK, g * LANE:(g + 1) * LANE])
                for g in range(SGU_GROUPS)]
        chunks.append(jnp.concatenate(cols, axis=1) + bs_ref[...])
    oa = (gu * jnp.concatenate(chunks, axis=0)).astype(BF16)
    pa_out[...] = (jax.nn.sigmoid(_dot(hb, win_ref[:, C_GA:C_GB])) * _dot(oa, wpa_ref[...])).astype(BF16)
    sgb_out[...] = jax.nn.sigmoid(_dot(hb, win_ref[:, C_GB:C_END])).astype(BF16)

    cqn = _rms(_dot(hb, win_ref[:, C_Q:C_KV]), gq_ref[...]).astype(BF16)
    nw = MLA_HEADS * LANE
    qn = _dot(cqn, wq_ref[:, :nw])
    qr = _dot(cqn, wq_ref[:, nw:2 * nw])
    qt = _dot(cqn, wq_ref[:, 2 * nw:])
    for h in range(MLA_HEADS):
        sl = slice(h * LANE, (h + 1) * LANE)
        q_out[:, h * HEAD_PAD:h * HEAD_PAD + LANE] = (qn[:, sl] * q_scale).astype(BF16)
        q_out[:, h * HEAD_PAD + LANE:(h + 1) * HEAD_PAD] = (
            (qr[:, sl] * cos + qt[:, sl] * sin) * q_scale).astype(BF16)

    _kv_path(hb, win_ref, gkv_ref, wkv_ref, cos, sin, k_out, v_out)


def _ctxkv_kernel(c_ref, mod_ref, gpre_ref, win_ref, gkv_ref, wkv_ref, k_out, v_out):
    hb =(_rms(c_ref[0], gpre_ref[...]) * (1.0 + mod_ref[0, 1:2, :]) + mod_ref[0, 0:1, :]).astype(BF16)
    _kv_path(hb, win_ref, gkv_ref, wkv_ref, None, None, k_out, v_out)


def _const(shape):
    nd = len(shape)
    return pl.BlockSpec(shape, lambda *_: (0,) * nd)


def _inproj_call(x2, mod3, gpre, win, gq, gkv, wq, wkv, gsgu, ws, bs, wpa, cos, sin, *, batch, seq, tm):
    n = batch * seq
    tiles = seq // tm
    hw = MLA_HEADS * HEAD_PAD
    row = lambda i: (i, 0)
    kern = functools.partial(_inproj_kernel, q_scale=float((QK_NOPE + QK_ROPE) ** -0.5 * LOG2E))
    return pl.pallas_call(
        kern,
        grid=(n // tm,),
        in_specs=[pl.BlockSpec((tm, D_MODEL), row),
                  pl.BlockSpec((1, N_MOD, D_MODEL), lambda i: (i // tiles, 0, 0)),
                  _const(gpre.shape), _const(win.shape), _const(gq.shape), _const(gkv.shape),
                  _const(wq.shape), _const(wkv.shape), _const(gsgu.shape), _const(ws.shape),
                  _const(bs.shape), _const(wpa.shape),
                  pl.BlockSpec((tm, LANE), lambda i: (i % tiles, 0)),
                  pl.BlockSpec((tm, LANE), lambda i: (i % tiles, 0))],
        out_specs=[pl.BlockSpec((tm, hw), row),
                   pl.BlockSpec((1, tm, hw), lambda i: (i // tiles, i % tiles, 0)),
                   pl.BlockSpec((1, tm, hw), lambda i: (i // tiles, i % tiles, 0)),
                   pl.BlockSpec((tm, D_MODEL), row),
                   pl.BlockSpec((tm, D_MODEL), row)],
        out_shape=[jax.ShapeDtypeStruct((n, hw), BF16),
                   jax.ShapeDtypeStruct((batch, seq, hw), BF16),
                   jax.ShapeDtypeStruct((batch, seq, hw), BF16),
                   jax.ShapeDtypeStruct((n, D_MODEL), BF16),
                   jax.ShapeDtypeStruct((n, D_MODEL), BF16)],
        compiler_params=_params("arbitrary"),
        name="inproj",
    )(x2, mod3, gpre, win, gq, gkv, wq, wkv, gsgu, ws, bs, wpa, cos, sin)


def _ctxkv_call(ctx, mod3, gpre, win, gkv, wkv, *, batch):
    ctx_len = ctx.shape[1]
    hw = MLA_HEADS * HEAD_PAD
    blk = pl.BlockSpec((1, ctx_len, hw), lambda b: (b, 0, 0))
    shp = jax.ShapeDtypeStruct((batch, ctx_len, hw), BF16)
    return pl.pallas_call(
        _ctxkv_kernel,
        grid=(batch,),
        in_specs=[pl.BlockSpec((1, ctx_len, D_MODEL), lambda b: (b, 0, 0)),
                  pl.BlockSpec((1, N_MOD, D_MODEL), lambda b: (batch, 0, 0)),
                  _const(gpre.shape), _const(win.shape), _const(gkv.shape), _const(wkv.shape)],
        out_specs=[blk, blk],
        out_shape=[shp, shp],
        compiler_params=_params("arbitrary"),
        name="ctxkv",
    )(ctx, mod3, gpre, win, gkv, wkv)


def _attn_kernel(q_ref, kx_ref, vx_ref, kc_ref, vc_ref, o_ref, *, tk):
    q = q_ref[...]
    tq = q.shape[0]
    m = jnp.full((tq, 1), -jnp.inf, F32)
    acc = jnp.zeros((tq, HEAD_PAD), F32)
    chunks = [(kx_ref, vx_ref, slice(c * tk, (c + 1) * tk)) for c in range(kx_ref.shape[1] // tk)]
    chunks.append((kc_ref, vc_ref, slice(0, kc_ref.shape[1])))
    for k_ref, v_ref, rows in chunks:
        s = lax.dot_general(q, k_ref[0, rows, :], NT_DIMS, preferred_element_type=F32)
        m_new = jnp.maximum(m, jnp.max(s, axis=1, keepdims=True))
        p = jnp.exp2(s - m_new).astype(BF16)
        acc = acc * jnp.exp2(m - m_new) + _dot(p, v_ref[0, rows, :])
        m = m_new
    o_ref[...] = (acc[:, :V_DIM] / acc[:, V_DIM:]).astype(BF16)


def _attn_call(q, k_x, v_x, k_c, v_c, *, batch, seq, tq, tk):
    ctx_len = k_c.shape[1]
    tiles = seq // tq
    per_head = lambda rows: pl.BlockSpec((1, rows, HEAD_PAD), lambda b, h, i: (b, 0, h))
    return pl.pallas_call(
        functools.partial(_attn_kernel, tk=tk),
        grid=(batch, MLA_HEADS, tiles),
        in_specs=[pl.BlockSpec((tq, HEAD_PAD), lambda b, h, i: (b * tiles + i, h)),
                  per_head(seq), per_head(seq), per_head(ctx_len), per_head(ctx_len)],
        out_specs=pl.BlockSpec((tq, V_DIM), lambda b, h, i: (b * tiles + i, h)),
        out_shape=jax.ShapeDtypeStruct((batch * seq, MLA_HEADS * V_DIM), BF16),
        compiler_params=_params("arbitrary", "arbitrary", "arbitrary"),
        name="attn",
    )(q, k_x, v_x, k_c, v_c)


def _post_kernel(x_ref, ob_ref, pa_ref, sgb_ref, mod_ref, wpb_ref, wo_ref, gpost_ref, gpre2_ref,
                 wpq_ref, k1_ref, k2_ref, x1_out, hx2_out, s1_out, s2_out):
    t = _dot(ob_ref[...], wpb_ref[...])
    mix = (pa_ref[...].astype(F32) + sgb_ref[...].astype(F32) * t).astype(BF16)
    y = _dot(mix, wo_ref[...])
    x1 = x_ref[...] + mod_ref[0, 2:3, :] * _rms(y, gpost_ref[...])
    x1_out[...] = x1
    hx2 = (_rms(x1, gpre2_ref[...]) * (1.0 + mod_ref[0, 4:5, :]) + mod_ref[0, 3:4, :]).astype(BF16)
    hx2_out[...] = hx2
    qp = _dot(hx2, wpq_ref[...]).astype(BF16)
    k1 = k1_ref[...]
    k2 = k2_ref[...]
    for h in range(PEER_HEADS):
        base = h * 2 * PEER_HALF
        s1_out[h] = lax.dot_general(k1, qp[:, base:base + PEER_HALF], NT_DIMS, preferred_element_type=F32)
        s2_out[h] = lax.dot_general(k2, qp[:, base + PEER_HALF:base + 2 * PEER_HALF], NT_DIMS,
                                    preferred_element_type=F32)


def _post_call(x2, ob, pa, sgb, mod3, wpb, wo, gpost, gpre2, wpq, k1, k2, *, seq, tm):
    n = x2.shape[0]
    tiles = seq // tm
    row = lambda i: (i, 0)
    tok = lambda i: (0, 0, i)
    return pl.pallas_call(
        _post_kernel,
        grid=(n // tm,),
        in_specs=[pl.BlockSpec((tm, D_MODEL), row), pl.BlockSpec((tm, D_MODEL), row),
                  pl.BlockSpec((tm, D_MODEL), row), pl.BlockSpec((tm, D_MODEL), row),
                  pl.BlockSpec((1, N_MOD, D_MODEL), lambda i: (i // tiles, 0, 0)),
                  _const(wpb.shape), _const(wo.shape), _const(gpost.shape), _const(gpre2.shape),
                  _const(wpq.shape), _const(k1.shape), _const(k2.shape)],
        out_specs=[pl.BlockSpec((tm, D_MODEL), row), pl.BlockSpec((tm, D_MODEL), row),
                   pl.BlockSpec((PEER_HEADS, N_KEYS, tm), tok), pl.BlockSpec((PEER_HEADS, N_KEYS, tm), tok)],
        out_shape=[jax.ShapeDtypeStruct((n, D_MODEL), F32), jax.ShapeDtypeStruct((n, D_MODEL), BF16),
                   jax.ShapeDtypeStruct((PEER_HEADS, N_KEYS, n), F32),
                   jax.ShapeDtypeStruct((PEER_HEADS, N_KEYS, n), F32)],
        compiler_params=_params("arbitrary"),
        name="post",
    )(x2, ob, pa, sgb, mod3, wpb, wo, gpost, gpre2, wpq, k1, k2)


SUB = 8


def _sort_pairs(n):
    pairs = []

    def merge(lo, hi, r):
        step = r * 2
        if step < hi - lo:
            merge(lo, hi, step)
            merge(lo + r, hi, step)
            pairs.extend((i, i + r) for i in range(lo + r, hi - r, step))
        else:
            pairs.append((lo, lo + r))

    def sort(lo, hi):
        if hi - lo >= 1:
            mid = lo + (hi - lo) // 2
            sort(lo, mid)
            sort(mid + 1, hi)
            merge(lo, hi, 1)

    sort(0, n - 1)
    return pairs


def _cmpx(v, i, j):
    a, b = v[i], v[j]
    if b is None:
        return
    if a is None:
        v[i], v[j] = b, None
        return
    v[i], v[j] = jnp.maximum(a, b), jnp.minimum(a, b)


def _top16_sorted(slabs):
    v = list(slabs) + [None] * (PEER_TOPK - len(slabs))
    for i, j in _sort_pairs(PEER_TOPK):
        _cmpx(v, i, j)
    for shift in (SUB // 2, SUB // 4, SUB // 8):
        rolled = [None if x is None else pltpu.roll(x, shift, axis=0) for x in v]
        merged = []
        for i in range(PEER_TOPK):
            a, b = v[i], rolled[PEER_TOPK - 1 - i]
            merged.append(b if a is None else (a if b is None else jnp.maximum(a, b)))
        v = merged
        d = PEER_TOPK // 2
        while d >= 1:
            for i in range(PEER_TOPK):
                if i & d == 0:
                    _cmpx(v, i, i + d)
            d //= 2
    return v


def _topk_kernel(s1_ref, s2_ref, r2_out, c1_out, coef_out, e2_out):
    nslab = N_KEYS // SUB
    tl = s1_ref.shape[2]
    row = lax.broadcasted_iota(jnp.int32, (SUB, tl), 0)
    row_is = [row == b for b in range(SUB)]

    def spread(vals):
        out = vals[SUB - 1]
        for b in range(SUB - 2, -1, -1):
            out = jnp.where(row_is[b], vals[b], out)
        return out

    def body(h, carry):
        s1 = [s1_ref[h, j * SUB:(j + 1) * SUB, :] for j in range(nslab)]
        s2 = [s2_ref[h, j * SUB:(j + 1) * SUB, :] for j in range(nslab)]
        t1 = _top16_sorted(s1)
        t2 = _top16_sorted(s2)

        t2_lo, t2_hi, t1_hi = spread(t2[:SUB]), spread(t2[SUB:]), spread(t1[SUB:])
        cand = [t1[0] + t2_lo, t1[0] + t2_hi] + [t1[a] + t2_lo for a in range(1, SUB)] + [t1_hi + t2[0]]
        best = _top16_sorted(cand)
        tau = best[PEER_TOPK - 1]
        z = jnp.ones_like(tau)
        for k in range(1, PEER_TOPK):
            z = z + jnp.exp(best[k] - best[0])

        def count(base, bs):
            return sum(jnp.where(base + t2[b] >= tau, 1.0, 0.0) for b in bs)

        extra = [count(t1[0], range(4, 16)), count(t1[1], range(4, 8)), count(t1[2], range(4, 5))]
        inv_z = 1.0 / z
        c1_parts, coef_parts, r2_parts, e2_parts = [], [], [], []
        for j in range(nslab):
            x = s1[j]
            c = count(x, range(4))
            for a in range(3):
                c = c + jnp.where(x == t1[a], extra[a], 0.0)
            c1_parts.append(c)
            coef_parts.append(jnp.exp(x - t1[0]) * inv_z)
            y = s2[j]
            m8 = y < t2[7]
            m4 = y < jnp.where(m8, t2[11], t2[3])
            m2 = y < jnp.where(m8, jnp.where(m4, t2[13], t2[9]), jnp.where(m4, t2[5], t2[1]))
            piv = jnp.where(m8,
                            jnp.where(m4, jnp.where(m2, t2[14], t2[12]), jnp.where(m2, t2[10], t2[8])),
                            jnp.where(m4, jnp.where(m2, t2[6], t2[4]), jnp.where(m2, t2[2], t2[0])))
            rank = (jnp.where(m8, 8.0, 0.0) + jnp.where(m4, 4.0, 0.0) + jnp.where(m2, 2.0, 0.0)
                    + jnp.where(y < piv, 1.0, 0.0))
            r2_parts.append(jnp.where(y < t2[PEER_TOPK - 1], float(PEER_TOPK), rank))
            e2_parts.append(jnp.exp(y - t2[0]))
        c1_out[h] = jnp.concatenate(c1_parts, axis=0)
        coef_out[h] = jnp.concatenate(coef_parts, axis=0)
        r2_out[h] = jnp.concatenate(r2_parts, axis=0).astype(BF16)
        e2_out[h] = jnp.concatenate(e2_parts, axis=0).astype(BF16)
        return carry

    lax.fori_loop(0, PEER_HEADS, body, 0)


def _topk_call(s1t, s2t, *, tl):
    n = s1t.shape[2]
    blk = pl.BlockSpec((PEER_HEADS, N_KEYS, tl), lambda i: (0, 0, i))
    shp = jax.ShapeDtypeStruct((PEER_HEADS, N_KEYS, n), F32)
    shp_b = jax.ShapeDtypeStruct((PEER_HEADS, N_KEYS, n), BF16)
    return pl.pallas_call(
        _topk_kernel,
        grid=(n // tl,),
        in_specs=[blk, blk],
        out_specs=[blk, blk, blk, blk],
        out_shape=[shp_b, shp, shp, shp_b],
        compiler_params=_params("arbitrary"),
        name="topk",
    )(s1t, s2t)


def _dense_kernel(hx2_ref, u_ref, vt_ref, r2_ref, e2_ref, c1_ref, coef_ref, x1_ref, mod_ref, gpost_ref,
                  o_ref, a_scr, w_scr, acc_scr, r2_scr, e2_scr):
    k = pl.program_id(1)
    te = u_ref.shape[0]
    tm = hx2_ref.shape[0]
    nt = N_KEYS // BF16_ROWS
    groups = te // N_KEYS
    zero = jnp.zeros((), BF16)

    @pl.when(k == 0)
    def _():
        acc_scr[...] = jnp.zeros_like(acc_scr)
        for h in range(PEER_HEADS):
            r2_scr[h] = r2_ref[h]
            e2_scr[h] = e2_ref[h]

    a_t = lax.dot_general(u_ref[...], hx2_ref[...], NT_DIMS, preferred_element_type=F32)
    a_scr[...] = _gelu_tanh(a_t).astype(BF16)
    for g in range(groups):
        ks = slice(g * N_KEYS, (g + 1) * N_KEYS)
        for c in range(tm // LANE):
            ls = slice(c * LANE, (c + 1) * LANE)
            gate = jnp.zeros((N_KEYS, LANE), BF16)
            for h in range(PEER_HEADS):
                cnt = jnp.broadcast_to(c1_ref[h, g:g + 1, ls], (BF16_ROWS, LANE)).astype(BF16)
                cf = jnp.broadcast_to(coef_ref[h, g:g + 1, ls], (BF16_ROWS, LANE)).astype(BF16)
                hit = r2_scr[h, :, ls] < jnp.tile(cnt, (nt, 1))
                gate = gate + jnp.where(hit, e2_scr[h, :, ls] * jnp.tile(cf, (nt, 1)), zero)
            w_scr[ks, ls] = a_scr[ks, ls] * gate
    acc_scr[...] += _dot(vt_ref[...], w_scr[...])

    @pl.when(k == pl.num_programs(1) - 1)
    def _():
        o_ref[...] = x1_ref[...] + mod_ref[0, 5:6, :] * _rms(acc_scr[...].T, gpost_ref[...])


def _dense_call(hx2, u_b, vt_b, r2, e2, c1, coef, x1, mod3, gpost2, *, seq, tm, te):
    n = hx2.shape[0]
    ne = u_b.shape[0]
    tiles = seq // tm
    tok = lambda i, k: (0, 0, i)
    rows = te // N_KEYS
    return pl.pallas_call(
        _dense_kernel,
        grid=(n // tm, ne // te),
        in_specs=[pl.BlockSpec((tm, D_MODEL), lambda i, k: (i, 0)),
                  pl.BlockSpec((te, D_MODEL), lambda i, k: (k, 0)),
                  pl.BlockSpec((None, D_MODEL, te), lambda i, k: (k, 0, 0)),
                  pl.BlockSpec((PEER_HEADS, N_KEYS, tm), tok),
                  pl.BlockSpec((PEER_HEADS, N_KEYS, tm), tok),
                  pl.BlockSpec((PEER_HEADS, rows, tm), lambda i, k: (0, k, i)),
                  pl.BlockSpec((PEER_HEADS, rows, tm), lambda i, k: (0, k, i)),
                  pl.BlockSpec((tm, D_MODEL), lambda i, k: (i, 0)),
                  pl.BlockSpec((1, N_MOD, D_MODEL), lambda i, k: (i // tiles, 0, 0)),
                  pl.BlockSpec((1, D_MODEL), lambda i, k: (0, 0))],
        out_specs=pl.BlockSpec((tm, D_MODEL), lambda i, k: (i, 0)),
        out_shape=jax.ShapeDtypeStruct((n, D_MODEL), F32),
        scratch_shapes=[pltpu.VMEM((te, tm), BF16), pltpu.VMEM((te, tm), BF16), pltpu.VMEM((D_MODEL, tm), F32),
                        pltpu.VMEM((PEER_HEADS, N_KEYS, tm), BF16), pltpu.VMEM((PEER_HEADS, N_KEYS, tm), BF16)],
        compiler_params=_params("arbitrary", "arbitrary"),
        name="dense",
    )(hx2, u_b, vt_b, r2, e2, c1, coef, x1, mod3, gpost2)


def _rot_cols(w):
    lead = w.shape[:-1]
    w4 = w.reshape(lead + (2, 2, QK_ROPE // 4))
    return jnp.concatenate([-w4[..., 1:2, :], w4[..., 0:1, :]], axis=-2).reshape(w.shape)


def _pad_lanes(w):
    return jnp.concatenate([w, jnp.zeros(w.shape[:-1] + (LANE - w.shape[-1],), w.dtype)], axis=-1)


def _rope_tables(seq, dtype):
    quarter = QK_ROPE // 4
    n_rows = seq // GRID_W
    freqs = ROPE_THETA ** (-jnp.arange(quarter, dtype=jnp.float32) / quarter)
    ang_r = jnp.arange(n_rows).astype(jnp.float32)[:, None] * freqs[None, :]
    ang_c = jnp.arange(GRID_W).astype(jnp.float32)[:, None] * freqs[None, :]

    def table(fn):
        by_row = jnp.broadcast_to(fn(ang_r)[:, None, :], (n_rows, GRID_W, quarter))
        by_col = jnp.broadcast_to(fn(ang_c)[None, :, :], (n_rows, GRID_W, quarter))
        t = jnp.concatenate([by_row, by_row, by_col, by_col], axis=-1).reshape(seq, QK_ROPE)
        return _pad_lanes(t).astype(dtype)

    return table(jnp.cos), table(jnp.sin)


def _tiles(seq):
    tm = 256 if seq % 256 == 0 else CHUNK
    tk = next(cand for cand in (1024, 512, 256, 128) if seq % cand == 0)
    big = 512 if seq % 512 == 0 else tm
    tq = 1024 if seq % 1024 == 0 else big
    return dict(tm=big, tq=tq, tk=tk, tl=256 if seq % 256 == 0 else LANE, td=big, te=2048)


def kernel(x, c, ctx, c_ctx, w_mod, b_mod, g_pre1, g_post1, g_pre2, g_post2, w_in, g_q, g_kv, w_uq, w_ukv,
           g_sgu, w_s, b_s, w_pa, w_pb, w_o, w_pq, k1, k2, u_exp, v_exp):
    batch, seq, d = x.shape
    ctx_len = ctx.shape[1]
    depth = w_mod.shape[0]
    assert depth == 1 and d == D_MODEL and batch + 1 <= 8
    assert seq % CHUNK == 0 and ctx_len % SUB == 0
    tl = _tiles(seq)
    l = 0

    cvec = jnp.zeros((8, d), F32).at[:batch].set(c).at[batch].set(c_ctx)
    mod = _mod_call(cvec, w_mod[l], b_mod[l][None, :])
    mod3 = mod[:batch + 1].reshape(batch + 1, N_MOD, d)

    wi = w_in[l]
    off_q = 2 * SGU_WIDTH
    off_kv = off_q + Q_RANK
    off_kr = off_kv + KV_RANK
    off_ga = off_kr + QK_ROPE
    w_kr = wi[:, off_kr:off_ga]
    win = jnp.concatenate([wi[:, :off_kr], _pad_lanes(w_kr), _pad_lanes(_rot_cols(w_kr)), wi[:, off_ga:]],
                          axis=1).astype(BF16)
    wq3 = w_uq[l].reshape(Q_RANK, MLA_HEADS, QK_NOPE + QK_ROPE)
    wq_rope = wq3[:, :, QK_NOPE:]
    wq = jnp.concatenate([wq3[:, :, :QK_NOPE].reshape(Q_RANK, -1),
                          _pad_lanes(wq_rope).reshape(Q_RANK, -1),
                          _pad_lanes(_rot_cols(wq_rope)).reshape(Q_RANK, -1)], axis=1).astype(BF16)
    wkv3 = w_ukv[l].reshape(KV_RANK, MLA_HEADS, QK_NOPE + V_DIM)
    wkv = jnp.concatenate([wkv3[:, :, :QK_NOPE].reshape(KV_RANK, -1),
                           wkv3[:, :, QK_NOPE:].reshape(KV_RANK, -1)], axis=1).astype(BF16)
    bs = jnp.repeat(b_s[l].T, SGU_WIDTH // SGU_GROUPS, axis=1)
    cos, sin = _rope_tables(seq, x.dtype)
    row = lambda g: g[l][None, :]

    x2 = x.reshape(batch * seq, d)
    q, k_x, v_x, pa, sgb = _inproj_call(
        x2, mod3, row(g_pre1), win, row(g_q), row(g_kv), wq, wkv, row(g_sgu), w_s[l].astype(BF16), bs,
        w_pa[l].astype(BF16), cos, sin, batch=batch, seq=seq, tm=tl["tm"])
    k_c, v_c = _ctxkv_call(ctx, mod3, row(g_pre1), win, row(g_kv), wkv, batch=batch)

    ob = _attn_call(q, k_x, v_x, k_c, v_c, batch=batch, seq=seq, tq=tl["tq"], tk=tl["tk"])

    x1, hx2, s1t, s2t = _post_call(
        x2, ob, pa, sgb, mod3, w_pb[l].astype(BF16), w_o[l].astype(BF16), row(g_post1), row(g_pre2),
        w_pq[l].astype(BF16), k1[l].astype(BF16), k2[l].astype(BF16), seq=seq, tm=tl["tm"])

    r2, c1, coef, e2 = _topk_call(s1t, s2t, tl=tl["tl"])

    te = tl["te"]
    vt_b = v_exp[l].astype(BF16).reshape(-1, te, d).transpose(0, 2, 1)
    out = _dense_call(hx2, u_exp[l].astype(BF16), vt_b, r2, e2, c1, coef, x1, mod3,
                      row(g_post2), seq=seq, tm=tl["td"], te=te)
    return out.reshape(batch, seq, d)
```
